```python
import math
import jax, jax.numpy as jnp
from jax import lax
import numpy as np

D_MODEL = 1024
BATCH = 4
SEQ = 4096
DEPTH = 1
DEC_BATCH = 32
DEC_SEQ = 4
PAST_LEN = 8192
PAGE_SIZE = 128

H_A = 8
HD_A = 64
W_A = H_A * HD_A
H_I = 8
D_I = 32
TOPK_MAX = 256
H_B = 4
HD_B = 64
W_B = H_B * 2 * HD_B
Q_BLOCK = 128
DEEPNORM_ALPHA = (2.0 * DEPTH) ** 0.25
DEEPNORM_BETA = (8.0 * DEPTH) ** -0.25
LN_EPS = 1e-5
RMS_EPS = 1e-5
NEG_INF = -1e30

SPLITS = (
    ('q_a', W_A), ('k_a', W_A), ('v_a', W_A),
    ('q_idx', H_I * D_I), ('k_idx', D_I), ('w_idx', H_I), ('z_a', W_A),
    ('q_b', W_B), ('k_b', W_B), ('v_b', W_B), ('z_b', W_B),
    ('g_a', D_MODEL), ('g_b', D_MODEL),
)
D_IN = sum(n for _, n in SPLITS)

kernel_name = 'hybrid_dsa_diffattn_gated_merge_step'


def _alibi_slopes(n):
    return jnp.asarray([2.0 ** (-8.0 * (i + 1) / n) for i in range(n)], dtype=jnp.float32)


def _layernorm(x, g, b):
    xf = x.astype(jnp.float32)
    mu = jnp.mean(xf, -1, keepdims=True)
    var = jnp.mean(jnp.square(xf - mu), -1, keepdims=True)
    return ((xf - mu) * lax.rsqrt(var + LN_EPS) * g.astype(jnp.float32) + b.astype(jnp.float32)).astype(x.dtype)


def _in_proj(x, w_in, b_in):
    B, T, _ = x.shape
    h = jnp.einsum('btd,de->bte', x, w_in) + b_in
    offs = np.cumsum([n for _, n in SPLITS])[:-1].tolist()
    p = dict(zip([nm for nm, _ in SPLITS], jnp.split(h, offs, axis=-1)))
    p['q_a'] = p['q_a'].reshape(B, T, H_A, HD_A)
    p['k_a'] = p['k_a'].reshape(B, T, H_A, HD_A)
    p['v_a'] = p['v_a'].reshape(B, T, H_A, HD_A)
    p['q_idx'] = p['q_idx'].reshape(B, T, H_I, D_I)
    p['w_idx'] = p['w_idx'] * (H_I ** -0.5)
    p['q_b'] = p['q_b'].reshape(B, T, H_B, 2 * HD_B)
    p['k_b'] = p['k_b'].reshape(B, T, H_B, 2 * HD_B)
    p['v_b'] = p['v_b'].reshape(B, T, H_B, 2 * HD_B)
    return p


def _indexer_scores(q_idx, w_idx, k_idx, q_pos, k_pos):
    s = jnp.einsum('bqhd,bkd->bqhk', q_idx.astype(jnp.float32), k_idx.astype(jnp.float32)) * (D_I ** -0.5)
    s = jnp.einsum('bqhk,bqh->bqk', jax.nn.relu(s), w_idx.astype(jnp.float32))
    return jnp.where(k_pos[None, None, :] <= q_pos[None, :, None], s, NEG_INF)


def _sparse_attn(q, k_sel, v_sel, idx, q_pos, slopes):
    s = jnp.einsum('bqhd,bqkhd->bhqk', q.astype(jnp.float32), k_sel.astype(jnp.float32)) * (HD_A ** -0.5)
    dist = (q_pos[None, :, None] - idx).astype(jnp.float32)
    s = s - slopes[None, :, None, None] * dist[:, None]
    s = jnp.where((idx <= q_pos[None, :, None])[:, None], s, NEG_INF)
    p = jax.nn.softmax(s, axis=-1)
    return jnp.einsum('bhqk,bqkhd->bqhd', p, v_sel.astype(jnp.float32)).astype(q.dtype)


def _diff_attn(q, k, v, q_pos, k_pos, lam, slopes, subln_g, lambda_init):
    B, Tq = q.shape[:2]
    L = k.shape[1]
    qf = q.astype(jnp.float32).reshape(B, Tq, H_B, 2, HD_B)
    kf = k.astype(jnp.float32).reshape(B, L, H_B, 2, HD_B)
    s = jnp.einsum('bqhmd,bkhmd->bhmqk', qf, kf) * (HD_B ** -0.5)
    dist = (q_pos[:, None] - k_pos[None, :]).astype(jnp.float32)
    s = s - slopes[None, :, None, None, None] * dist
    s = jnp.where(k_pos[None, :] <= q_pos[:, None], s, NEG_INF)
    p = jax.nn.softmax(s, axis=-1)
    a = p[:, :, 0] - lam * p[:, :, 1]
    o = jnp.einsum('bhqk,bkhe->bqhe', a, v.astype(jnp.float32))
    o = o * lax.rsqrt(jnp.mean(o * o, -1, keepdims=True) + RMS_EPS) * subln_g.astype(jnp.float32)
    return (o * (1.0 - lambda_init)).astype(q.dtype)


def _blocks(a):
    B, T = a.shape[:2]
    return jnp.moveaxis(a.reshape((B, T // Q_BLOCK, Q_BLOCK) + a.shape[2:]), 1, 0)


def _unblocks(a):
    nb, B, Q = a.shape[:3]
    return jnp.moveaxis(a, 0, 1).reshape((B, nb * Q) + a.shape[3:])


def _merge_out(x, o_a, o_b, p, w_pa, w_pb, w_out, ln_g, ln_b):
    B, T, _ = x.shape
    ya = jnp.einsum('btc,cd->btd', o_a.reshape(B, T, W_A) * jax.nn.silu(p['z_a']), w_pa)
    yb = jnp.einsum('btc,cd->btd', o_b.reshape(B, T, W_B) * jax.nn.silu(p['z_b']), w_pb)
    m = jax.nn.sigmoid(p['g_a']) * ya + jax.nn.sigmoid(p['g_b']) * yb
    y = jnp.einsum('btc,cd->btd', m, w_out)
    return _layernorm(DEEPNORM_ALPHA * x + y, ln_g, ln_b)


def _lambda(lq1, lk1, lq2, lk2, lambda_init):
    f = jnp.float32
    return (jnp.exp(jnp.sum(lq1.astype(f) * lk1.astype(f))) - jnp.exp(jnp.sum(lq2.astype(f) * lk2.astype(f))) + lambda_init)


def _prompt_layer(x, w_in, b_in, lam, lambda_init, subln_g, w_pa, w_pb, w_out, ln_g, ln_b):
    B, T, _ = x.shape
    p = _in_proj(x, w_in, b_in)
    k_pos = jnp.arange(T, dtype=jnp.int32)
    top_k = min(TOPK_MAX, T // 4)
    sl_a, sl_b = _alibi_slopes(H_A), _alibi_slopes(H_B)
    starts = jnp.arange(T // Q_BLOCK, dtype=jnp.int32) * Q_BLOCK

    def a_block(args):
        q, qi, wi, start = args
        q_pos = start + jnp.arange(Q_BLOCK, dtype=jnp.int32)
        sc = _indexer_scores(qi, wi, p['k_idx'], q_pos, k_pos)
        _, idx = lax.top_k(sc, top_k)
        k_sel = jax.vmap(lambda kb, ib: kb[ib])(p['k_a'], idx)
        v_sel = jax.vmap(lambda vb, ib: vb[ib])(p['v_a'], idx)
        return _sparse_attn(q, k_sel, v_sel, idx, q_pos, sl_a)

    def b_block(args):
        q, start = args
        q_pos = start + jnp.arange(Q_BLOCK, dtype=jnp.int32)
        return _diff_attn(q, p['k_b'], p['v_b'], q_pos, k_pos, lam, sl_b, subln_g, lambda_init)

    o_a = _unblocks(lax.map(a_block, (_blocks(p['q_a']), _blocks(p['q_idx']), _blocks(p['w_idx']), starts)))
    o_b = _unblocks(lax.map(b_block, (_blocks(p['q_b']), starts)))
    y = _merge_out(x, o_a, o_b, p, w_pa, w_pb, w_out, ln_g, ln_b)
    return y, p


def _sample_layer(x, c_k_a, c_v_a, c_k_idx, c_k_b, c_v_b, page_table, w_in, b_in, lam, lambda_init, subln_g, w_pa, w_pb, w_out, ln_g, ln_b):
    B, T, _ = x.shape
    p = _in_proj(x, w_in, b_in)
    L = PAST_LEN + T
    q_pos = PAST_LEN + jnp.arange(T, dtype=jnp.int32)
    k_pos = jnp.arange(L, dtype=jnp.int32)
    top_k = min(TOPK_MAX, L // 4)
    sl_a, sl_b = _alibi_slopes(H_A), _alibi_slopes(H_B)

    def paged(pool):
        g = pool[page_table]
        return g.reshape((B, PAST_LEN) + pool.shape[2:])

    k_idx_all = jnp.concatenate([paged(c_k_idx), p['k_idx']], axis=1)
    sc = _indexer_scores(p['q_idx'], p['w_idx'], k_idx_all, q_pos, k_pos)
    _, idx = lax.top_k(sc, top_k)
    in_past = idx < PAST_LEN
    pidx = jnp.minimum(idx, PAST_LEN - 1)
    phys = jax.vmap(lambda pt, i: pt[i])(page_table, pidx // PAGE_SIZE)
    off = pidx % PAGE_SIZE
    nidx = jnp.clip(idx - PAST_LEN, 0, T - 1)
    sel_mask = in_past[..., None, None]

    def gather_rows(pool, new):
        past_rows = pool[phys, off]
        new_rows = jax.vmap(lambda nb, i: nb[i])(new, nidx)
        return jnp.where(sel_mask, past_rows, new_rows)

    k_sel = gather_rows(c_k_a, p['k_a'])
    v_sel = gather_rows(c_v_a, p['v_a'])
    o_a = _sparse_attn(p['q_a'], k_sel, v_sel, idx, q_pos, sl_a)

    k_b_all = jnp.concatenate([paged(c_k_b), p['k_b']], axis=1)
    v_b_all = jnp.concatenate([paged(c_v_b), p['v_b']], axis=1)
    o_b = _diff_attn(p['q_b'], k_b_all, v_b_all, q_pos, k_pos, lam, sl_b, subln_g, lambda_init)

    y = _merge_out(x, o_a, o_b, p, w_pa, w_pb, w_out, ln_g, ln_b)
    return y, p


def setup_inputs(seed: int = 0) -> dict:
    key = jax.random.key(seed)
    ks = jax.random.split(key, 24)
    n_pages = PAST_LEN // PAGE_SIZE
    n_used = DEC_BATCH * n_pages
    n_pool = n_used + max(1, n_used // 4)
    nrm = jax.random.normal
    f = jnp.float32
    scale = np.ones((D_IN,), np.float32)
    off = 0
    for name, n in SPLITS:
        if name in ('v_a', 'v_b'):
            scale[off:off + n] = DEEPNORM_BETA
        off += n
    page_table = jax.random.permutation(ks[0], n_pool)[:n_used].reshape(DEC_BATCH, n_pages).astype(jnp.int32)
    return {
        'x_prompt': nrm(ks[1], (BATCH, SEQ, D_MODEL), f),
        'x_sample': nrm(ks[2], (DEC_BATCH, DEC_SEQ, D_MODEL), f),
        'cache_k_a': nrm(ks[3], (DEPTH, n_pool, PAGE_SIZE, H_A, HD_A), f),
        'cache_v_a': nrm(ks[4], (DEPTH, n_pool, PAGE_SIZE, H_A, HD_A), f) * DEEPNORM_BETA,
        'cache_k_idx': nrm(ks[5], (DEPTH, n_pool, PAGE_SIZE, D_I), f),
        'cache_k_b': nrm(ks[6], (DEPTH, n_pool, PAGE_SIZE, H_B, 2 * HD_B), f),
        'cache_v_b': nrm(ks[7], (DEPTH, n_pool, PAGE_SIZE, H_B, 2 * HD_B), f) * DEEPNORM_BETA,
        'page_table': page_table,
        'w_in': nrm(ks[8], (DEPTH, D_MODEL, D_IN), f) * (D_MODEL ** -0.5) * jnp.asarray(scale),
        'b_in': nrm(ks[9], (DEPTH, D_IN), f) * 0.02,
        'lambda_q1': nrm(ks[10], (DEPTH, HD_B), f) * 0.1,
        'lambda_k1': nrm(ks[11], (DEPTH, HD_B), f) * 0.1,
        'lambda_q2': nrm(ks[12], (DEPTH, HD_B), f) * 0.1,
        'lambda_k2': nrm(ks[13], (DEPTH, HD_B), f) * 0.1,
        'subln_g': 1.0 + 0.02 * nrm(ks[14], (DEPTH, 2 * HD_B), f),
        'w_pa': nrm(ks[15], (DEPTH, W_A, D_MODEL), f) * (W_A ** -0.5) * DEEPNORM_BETA,
        'w_pb': nrm(ks[16], (DEPTH, W_B, D_MODEL), f) * (W_B ** -0.5) * DEEPNORM_BETA,
        'w_out': nrm(ks[17], (DEPTH, D_MODEL, D_MODEL), f) * (D_MODEL ** -0.5) * DEEPNORM_BETA,
        'ln_g': 1.0 + 0.02 * nrm(ks[18], (DEPTH, D_MODEL), f),
        'ln_b': 0.02 * nrm(ks[19], (DEPTH, D_MODEL), f),
    }


def reference(x_prompt, x_sample, cache_k_a, cache_v_a, cache_k_idx, cache_k_b, cache_v_b, page_table,
              w_in, b_in, lambda_q1, lambda_k1, lambda_q2, lambda_k2, subln_g, w_pa, w_pb, w_out, ln_g, ln_b):
    xp, xs = x_prompt, x_sample
    pk_a, pv_a, pk_idx, pk_b, pv_b = [], [], [], [], []
    sk_a, sv_a, sk_idx, sk_b, sv_b = [], [], [], [], []
    for l in range(DEPTH):
        lambda_init = 0.8 - 0.6 * math.exp(-0.3 * l)
        lam = _lambda(lambda_q1[l], lambda_k1[l], lambda_q2[l], lambda_k2[l], lambda_init)
        xp_new, pp = _prompt_layer(xp, w_in[l], b_in[l], lam, lambda_init, subln_g[l], w_pa[l], w_pb[l], w_out[l], ln_g[l], ln_b[l])
        xs_new, ps = _sample_layer(xs, cache_k_a[l], cache_v_a[l], cache_k_idx[l], cache_k_b[l], cache_v_b[l], page_table,
                                   w_in[l], b_in[l], lam, lambda_init, subln_g[l], w_pa[l], w_pb[l], w_out[l], ln_g[l], ln_b[l])
        pk_a.append(pp['k_a']); pv_a.append(pp['v_a']); pk_idx.append(pp['k_idx']); pk_b.append(pp['k_b']); pv_b.append(pp['v_b'])
        sk_a.append(ps['k_a']); sv_a.append(ps['v_a']); sk_idx.append(ps['k_idx']); sk_b.append(ps['k_b']); sv_b.append(ps['v_b'])
        xp, xs = xp_new, xs_new
    return (xp, xs,
            jnp.stack(pk_a), jnp.stack(pv_a), jnp.stack(pk_idx), jnp.stack(pk_b), jnp.stack(pv_b),
            jnp.stack(sk_a), jnp.stack(sv_a), jnp.stack(sk_idx), jnp.stack(sk_b), jnp.stack(sv_b))
```

```python
import functools
import math

import jax
import jax.numpy as jnp
from jax import lax
from jax.experimental import pallas as pl
from jax.experimental.pallas import tpu as pltpu

F32 = jnp.float32
BF16 = jnp.bfloat16
I32 = jnp.int32

H_A, HD_A = 8, 64
H_I, D_I = 8, 32
H_B, HD_B = 4, 64
TOPK_MAX = 256
PAGE = 128
LN_EPS = 1e-5
RMS_EPS = 1e-5
NEG = -1e30
PAD_SCORE = -3e38

W_A = H_A * HD_A
W_B = H_B * 2 * HD_B
LANES = 128
VMEM_LIMIT = 56 * 1024 * 1024

SPLITS = (
    ('q_a', W_A), ('k_a', W_A), ('v_a', W_A),
    ('q_idx', H_I * D_I), ('k_idx', D_I), ('w_idx', H_I), ('z_a', W_A),
    ('q_b', W_B), ('k_b', W_B), ('v_b', W_B), ('z_b', W_B),
    ('g_a', None), ('g_b', None),
)


def _slopes(n):
    return tuple(2.0 ** (-8.0 * (i + 1) / n) for i in range(n))


def _cparams(sem):
    return pltpu.CompilerParams(dimension_semantics=sem, vmem_limit_bytes=VMEM_LIMIT)


def _dot_nt(a, b):
    return lax.dot_general(a, b, (((1,), (1,)), ((), ())), preferred_element_type=F32)


def _dot(a, b):
    return jnp.dot(a, b, preferred_element_type=F32)


_C_QA, _C_KA, _C_VA, _C_QB, _C_KB, _C_VB = 0, 512, 1024, 1536, 2048, 2560
_C_QI, _C_KI, _C_WI, _C_END = 3072, 3328, 3456, 3584


def _proj_kernel(x_ref, w_ref, b_ref, qa_ref, ka32_ref, ka16_ref, va32_ref, va16_ref,
                 qb_ref, kb32_ref, kb16_ref, vb32_ref, vb16_ref, qi_ref, ki32_ref, wi_ref):
    x = x_ref[...].astype(BF16)

    def proj(c0, n):
        return _dot(x, w_ref[:, c0:c0 + n]) + b_ref[:, c0:c0 + n]

    qa_ref[...] = (proj(_C_QA, W_A) * (HD_A ** -0.5)).astype(BF16)
    t = proj(_C_KA, W_A)
    ka32_ref[...] = t
    ka16_ref[...] = t.astype(BF16)
    t = proj(_C_VA, W_A)
    va32_ref[...] = t
    va16_ref[...] = t.astype(BF16)
    qb_ref[...] = (proj(_C_QB, W_B) * (HD_B ** -0.5)).astype(BF16)
    t = proj(_C_KB, W_B)
    kb32_ref[...] = t
    kb16_ref[...] = t.astype(BF16)
    t = proj(_C_VB, W_B)
    vb32_ref[...] = t
    vb16_ref[...] = t.astype(BF16)
    qi_ref[...] = proj(_C_QI, H_I * D_I).astype(BF16)
    t = proj(_C_KI, LANES)
    ki32_ref[...] = t[:, :D_I]
    t = proj(_C_WI, LANES)
    wi_ref[...] = t[:, :H_I] * ((H_I ** -0.5) * (D_I ** -0.5))


def _project(x2d, w1, b1, tm):
    n = x2d.shape[0]
    d = x2d.shape[1]
    row = lambda w: pl.BlockSpec((tm, w), lambda i: (i, 0))
    full = lambda a: pl.BlockSpec(a.shape, lambda i: (0, 0))
    sds = jax.ShapeDtypeStruct
    outs = [
        (W_A, BF16), (W_A, F32), (W_A, BF16), (W_A, F32), (W_A, BF16),
        (W_B, BF16), (W_B, F32), (W_B, BF16), (W_B, F32), (W_B, BF16),
        (H_I * D_I, BF16), (D_I, F32), (H_I, F32),
    ]
    return pl.pallas_call(
        _proj_kernel,
        grid=(n // tm,),
        in_specs=[pl.BlockSpec((tm, d), lambda i: (i, 0)), full(w1), full(b1)],
        out_specs=[row(w) for w, _ in outs],
        out_shape=[sds((n, w), dt) for w, dt in outs],
        compiler_params=_cparams(("parallel",)),
        name="proj",
    )(x2d, w1, b1)


def _float_of_key(u):
    s = u ^ jnp.int32(-2 ** 31)
    fb = jnp.where(s >= 0, s, s ^ jnp.int32(0x7FFFFFFF))
    return lax.bitcast_convert_type(fb, F32)


def _kth_largest(count_ge, shape, k):
    def body(i, prefix):
        bit = lax.shift_left(jnp.int32(1), jnp.int32(31) - i)
        cand = prefix | bit
        cnt = count_ge(_float_of_key(cand))
        return jnp.where(cnt >= k, cand, prefix)

    prefix = lax.fori_loop(0, 32, body, jnp.zeros(shape, I32))
    return _float_of_key(prefix)


def _prompt_a_kernel(q_ref, k_ref, vt_ref, qit_ref, wt_ref, ki_ref, o_ref,
                     sc_scr, acc_scr, ot_scr, *, tq, topk):
    qi = pl.program_id(1)
    nkb = qi + 1
    tk = tq
    kl = lax.broadcasted_iota(I32, (tk, tq), 0)
    ql = lax.broadcasted_iota(I32, (tk, tq), 1)
    causal = kl <= ql
    wt = wt_ref[0]

    def idx_body(kb, c):
        ki = ki_ref[0, pl.ds(pl.multiple_of(kb * tk, tk), tk), :]
        acc = jnp.zeros((tk, tq), F32)
        for h in range(H_I):
            r = _dot(ki, qit_ref[0, h])
            acc = acc + jnp.maximum(r, 0.0) * wt[h:h + 1, :]
        sc_scr[kb] = acc
        return c

    lax.fori_loop(0, nkb, idx_body, 0)
    sc_scr[qi] = jnp.where(causal, sc_scr[qi], NEG)

    def count_ge(v):
        def body(kb, cnt):
            return cnt + jnp.sum(jnp.where(sc_scr[kb] >= v, 1.0, 0.0), axis=0, keepdims=True)
        return lax.fori_loop(0, nkb, body, jnp.zeros((1, tq), F32))

    thr = _kth_largest(count_ge, (1, tq), float(topk))

    def sel_body(kb, c):
        sc_scr[kb] = jnp.where(sc_scr[kb] >= thr, 0.0, NEG)
        return c

    lax.fori_loop(0, nkb, sel_body, 0)
    sc_scr[qi] = jnp.where(causal, sc_scr[qi], NEG)

    klf = kl.astype(F32)
    lane = lax.broadcasted_iota(I32, (tq, LANES), 1)
    slopes = _slopes(H_A)
    for h in range(H_A):
        pair, odd = h // 2, h % 2
        cols = slice(pair * LANES, (pair + 1) * LANES)
        qp = q_ref[0, :, cols]
        qm = jnp.where((lane >= HD_A) == bool(odd), qp, jnp.zeros_like(qp))
        a_h = slopes[h] * klf
        acc_scr[...] = jnp.zeros_like(acc_scr)

        def body(kb, carry, cols=cols, qm=qm, a_h=a_h, slope=slopes[h]):
            m, l = carry
            kblk = k_ref[0, pl.ds(pl.multiple_of(kb * tk, tk), tk), cols]
            s = _dot_nt(kblk, qm) + sc_scr[kb] + a_h
            c = slope * ((kb - qi) * tk + jnp.zeros((1, tq), I32)).astype(F32)
            m_new = jnp.maximum(m, jnp.max(s, axis=0, keepdims=True) + c)
            p = jnp.exp(s - (m_new - c))
            alpha = jnp.exp(m - m_new)
            l = alpha * l + jnp.sum(p, axis=0, keepdims=True)
            vt = vt_ref[0, kb, cols, :]
            acc_scr[...] = acc_scr[...] * alpha + _dot(vt, p.astype(BF16))
            return m_new, l

        m, l = lax.fori_loop(0, nkb, body,
                             (jnp.full((1, tq), NEG, F32), jnp.zeros((1, tq), F32)))
        o = acc_scr[...] / l
        ot_scr[h * HD_A:(h + 1) * HD_A, :] = o[odd * HD_A:(odd + 1) * HD_A, :]
    o_ref[0] = ot_scr[...].T.astype(o_ref.dtype)


def _prompt_a(qa, ka, vta, qit, wt, ki, tq, topk):
    b, t, _ = qa.shape
    nq = t // tq
    kern = functools.partial(_prompt_a_kernel, tq=tq, topk=topk)
    return pl.pallas_call(
        kern,
        grid=(b, nq),
        in_specs=[
            pl.BlockSpec((1, tq, W_A), lambda i, j: (i, j, 0)),
            pl.BlockSpec((1, t, W_A), lambda i, j: (i, 0, 0)),
            pl.BlockSpec((1, nq, W_A, tq), lambda i, j: (i, 0, 0, 0)),
            pl.BlockSpec((1, H_I, D_I, tq), lambda i, j: (i, 0, 0, j)),
            pl.BlockSpec((1, H_I, tq), lambda i, j: (i, 0, j)),
            pl.BlockSpec((1, t, D_I), lambda i, j: (i, 0, 0)),
        ],
        out_specs=pl.BlockSpec((1, tq, W_A), lambda i, j: (i, j, 0)),
        out_shape=jax.ShapeDtypeStruct((b, t, W_A), F32),
        scratch_shapes=[
            pltpu.VMEM((nq, tq, tq), F32),
            pltpu.VMEM((LANES, tq), F32),
            pltpu.VMEM((W_A, tq), F32),
        ],
        compiler_params=_cparams(("parallel", "arbitrary")),
        name="prompt_a",
    )(qa, ka, vta, qit, wt, ki)


def _lambda_of(lq1_ref, lk1_ref, lq2_ref, lk2_ref, lambda_init):
    s1 = jnp.sum(lq1_ref[...] * lk1_ref[...], axis=1, keepdims=True)
    s2 = jnp.sum(lq2_ref[...] * lk2_ref[...], axis=1, keepdims=True)
    return jnp.exp(s1) - jnp.exp(s2) + lambda_init


def _prompt_b_kernel(q_ref, k_ref, vt_ref, lq1_ref, lk1_ref, lq2_ref, lk2_ref, g_ref, o_ref,
                     acc1_scr, acc2_scr, ot_scr, *, tq, lambda_init):
    qi = pl.program_id(1)
    nkb = qi + 1
    tk = tq
    kl = lax.broadcasted_iota(I32, (tk, tq), 0)
    ql = lax.broadcasted_iota(I32, (tk, tq), 1)
    causal = kl <= ql
    klf = kl.astype(F32)
    lane = lax.broadcasted_iota(I32, (tq, LANES), 1)
    lam = _lambda_of(lq1_ref, lk1_ref, lq2_ref, lk2_ref, lambda_init)
    g = g_ref[...]
    slopes = _slopes(H_B)
    neg0 = jnp.full((1, tq), NEG, F32)
    zero = jnp.zeros((1, tq), F32)
    for h in range(H_B):
        cols = slice(h * LANES, (h + 1) * LANES)
        qp = q_ref[0, :, cols]
        q1 = jnp.where(lane < HD_B, qp, jnp.zeros_like(qp))
        q2 = jnp.where(lane >= HD_B, qp, jnp.zeros_like(qp))
        a_h = slopes[h] * klf
        a_d = jnp.where(causal, a_h, NEG)
        acc1_scr[...] = jnp.zeros_like(acc1_scr)
        acc2_scr[...] = jnp.zeros_like(acc2_scr)

        def step(kb, carry, bias, cols=cols, q1=q1, q2=q2, slope=slopes[h]):
            m1, l1, m2, l2 = carry
            kblk = k_ref[0, pl.ds(pl.multiple_of(kb * tk, tk), tk), cols]
            vt = vt_ref[0, kb, cols, :]
            c = slope * ((kb - qi) * tk + jnp.zeros((1, tq), I32)).astype(F32)

            def one(qm, m, l, acc_ref):
                s = _dot_nt(kblk, qm) + bias
                m_new = jnp.maximum(m, jnp.max(s, axis=0, keepdims=True) + c)
                p = jnp.exp(s - (m_new - c))
                alpha = jnp.exp(m - m_new)
                l = alpha * l + jnp.sum(p, axis=0, keepdims=True)
                acc_ref[...] = acc_ref[...] * alpha + _dot(vt, p.astype(BF16))
                return m_new, l

            m1, l1 = one(q1, m1, l1, acc1_scr)
            m2, l2 = one(q2, m2, l2, acc2_scr)
            return m1, l1, m2, l2

        carry = lax.fori_loop(0, qi, functools.partial(step, bias=a_h),
                              (neg0, zero, neg0, zero))
        m1, l1, m2, l2 = step(qi, carry, a_d)
        o = acc1_scr[...] / l1 - lam * (acc2_scr[...] / l2)
        ms = jnp.mean(o * o, axis=0, keepdims=True)
        o = o * lax.rsqrt(ms + RMS_EPS) * g
        ot_scr[h * LANES:(h + 1) * LANES, :] = o * (1.0 - lambda_init)
    o_ref[0] = ot_scr[...].T.astype(o_ref.dtype)


def _prompt_b(qb, kb, vtb, lq1, lk1, lq2, lk2, g_col, tq, lambda_init):
    b, t, _ = qb.shape
    nq = t // tq
    kern = functools.partial(_prompt_b_kernel, tq=tq, lambda_init=lambda_init)
    small = lambda a: pl.BlockSpec(a.shape, lambda i, j: (0, 0))
    return pl.pallas_call(
        kern,
        grid=(b, nq),
        in_specs=[
            pl.BlockSpec((1, tq, W_B), lambda i, j: (i, j, 0)),
            pl.BlockSpec((1, t, W_B), lambda i, j: (i, 0, 0)),
            pl.BlockSpec((1, nq, W_B, tq), lambda i, j: (i, 0, 0, 0)),
            small(lq1), small(lk1), small(lq2), small(lk2), small(g_col),
        ],
        out_specs=pl.BlockSpec((1, tq, W_B), lambda i, j: (i, j, 0)),
        out_shape=jax.ShapeDtypeStruct((b, t, W_B), F32),
        scratch_shapes=[
            pltpu.VMEM((LANES, tq), F32),
            pltpu.VMEM((LANES, tq), F32),
            pltpu.VMEM((W_B, tq), F32),
        ],
        compiler_params=_cparams(("parallel", "arbitrary")),
        name="prompt_b",
    )(qb, kb, vtb, lq1, lk1, lq2, lk2, g_col)


def _sigmoid(x):
    return 1.0 / (1.0 + jnp.exp(-x))


def _merge_kernel(x_ref, oa_ref, ob_ref, wz_ref, bz_ref, wpa_ref, wpb_ref, wout_ref,
                  lng_ref, lnb_ref, y_ref, *, alpha):
    x = x_ref[...]
    xh = x.astype(BF16)

    def proj(c0, n):
        return _dot(xh, wz_ref[:, c0:c0 + n]) + bz_ref[:, c0:c0 + n]

    d = x.shape[1]
    za = proj(0, W_A)
    ua = (oa_ref[...] * (za * _sigmoid(za))).astype(BF16)
    ya = _dot(ua, wpa_ref[...])
    zb = proj(W_A, W_B)
    ub = (ob_ref[...] * (zb * _sigmoid(zb))).astype(BF16)
    yb = _dot(ub, wpb_ref[...])
    m = _sigmoid(proj(W_A + W_B, d)) * ya + _sigmoid(proj(W_A + W_B + d, d)) * yb
    r = alpha * x + _dot(m.astype(BF16), wout_ref[...])
    mu = jnp.mean(r, axis=1, keepdims=True)
    rc = r - mu
    var = jnp.mean(rc * rc, axis=1, keepdims=True)
    y_ref[...] = rc * lax.rsqrt(var + LN_EPS) * lng_ref[...] + lnb_ref[...]


def _merge(x2d, oa, ob, wz, bz, wpa, wpb, wout, lng, lnb, tm, alpha):
    n, d = x2d.shape
    row = lambda w: pl.BlockSpec((tm, w), lambda i: (i, 0))
    full = lambda a: pl.BlockSpec(a.shape, lambda i: (0, 0))
    return pl.pallas_call(
        functools.partial(_merge_kernel, alpha=alpha),
        grid=(n // tm,),
        in_specs=[row(d), row(W_A), row(W_B), full(wz), full(bz), full(wpa), full(wpb),
                  full(wout), full(lng), full(lnb)],
        out_specs=row(d),
        out_shape=jax.ShapeDtypeStruct((n, d), F32),
        compiler_params=_cparams(("parallel",)),
        name="merge",
    )(x2d, oa, ob, wz, bz, wpa, wpb, wout, lng, lnb)


def _sample_idx_kernel(pt_ref, qi_ref, wcol_ref, kip_ref, kin_ref, bias_ref, sc_scr,
                       *, n_pages, n_tok, topk):
    p = pl.program_id(1)
    rows = n_tok * H_I
    lane = lax.broadcasted_iota(I32, (8, PAGE), 1)
    trow = lax.broadcasted_iota(I32, (8, PAGE), 0)

    def scores(kpage):
        r = _dot_nt(qi_ref[0], kpage.astype(BF16))
        r = jnp.maximum(r, 0.0) * wcol_ref[0]
        per_tok = [jnp.sum(r[t * H_I:(t + 1) * H_I, :], axis=0, keepdims=True)
                   for t in range(n_tok)]
        pad = [jnp.zeros((1, PAGE), F32)] * (8 - n_tok)
        return jnp.concatenate(per_tok + pad, axis=0)

    @pl.when(p < n_pages)
    def _():
        sc_scr[p] = scores(kip_ref[0])

    @pl.when(p == n_pages)
    def _():
        s = scores(kin_ref[0])
        s = jnp.where(lane <= trow, s, NEG)
        sc_scr[n_pages] = jnp.where(lane < n_tok, s, PAD_SCORE)

        def count_ge(v):
            cnt = jnp.zeros((8, PAGE), F32)
            for j in range(n_pages + 1):
                cnt = cnt + jnp.where(sc_scr[j] >= v, 1.0, 0.0)
            return jnp.sum(cnt, axis=1, keepdims=True)

        thr = _kth_largest(count_ge, (8, 1), float(topk))
        for j in range(n_pages + 1):
            sel = sc_scr[j] >= thr
            if j == n_pages:
                sel = sel & (lane <= trow) & (lane < n_tok)
            bias_ref[0, j] = jnp.where(sel, 0.0, NEG)


def _sample_idx(page_table, qi_rows, wcol, cache_ki, ki_new, n_tok, topk):
    b, n_pages = page_table.shape
    rows = n_tok * H_I
    kern = functools.partial(_sample_idx_kernel, n_pages=n_pages, n_tok=n_tok, topk=topk)
    last = n_pages - 1
    grid_spec = pltpu.PrefetchScalarGridSpec(
        num_scalar_prefetch=1,
        grid=(b, n_pages + 1),
        in_specs=[
            pl.BlockSpec((1, rows, D_I), lambda i, p, pt: (i, 0, 0)),
            pl.BlockSpec((1, rows, 1), lambda i, p, pt: (i, 0, 0)),
            pl.BlockSpec((1, PAGE, D_I), lambda i, p, pt: (pt[i, jnp.minimum(p, last)], 0, 0)),
            pl.BlockSpec((1, PAGE, D_I), lambda i, p, pt: (i, 0, 0)),
        ],
        out_specs=pl.BlockSpec((1, n_pages + 1, 8, PAGE), lambda i, p, pt: (i, 0, 0, 0)),
        scratch_shapes=[pltpu.VMEM((n_pages + 1, 8, PAGE), F32)],
    )
    return pl.pallas_call(
        kern,
        grid_spec=grid_spec,
        out_shape=jax.ShapeDtypeStruct((b, n_pages + 1, 8, PAGE), F32),
        compiler_params=_cparams(("parallel", "arbitrary")),
        name="sample_idx",
    )(page_table, qi_rows, wcol, cache_ki, ki_new)


def _decode_kernel(pt_ref, wq_ref, slope_ref, kp_ref, vp_ref, kn_ref, vn_ref, *rest,
                   n_pages, n_tok, past_len, group, with_bias):
    if with_bias:
        bias_ref, o_ref, acc_scr, m_scr, l_scr = rest
    else:
        bias_ref = None
        o_ref, acc_scr, m_scr, l_scr = rest
    p = pl.program_id(1)
    rows = n_tok * 8
    lane = lax.broadcasted_iota(I32, (rows, PAGE), 1)
    trow = lax.broadcasted_iota(I32, (rows, PAGE), 0) // 8

    @pl.when(p == 0)
    def _():
        acc_scr[...] = jnp.zeros_like(acc_scr)
        m_scr[...] = jnp.full_like(m_scr, NEG)
        l_scr[...] = jnp.zeros_like(l_scr)

    def update(kpage, vpage, kpos, valid):
        s = _dot_nt(wq_ref[0], kpage.astype(BF16))
        dist = (past_len + trow - kpos).astype(F32)
        s = s - slope_ref[...] * dist
        if with_bias:
            b8 = bias_ref[0, 0]
            s = s + jnp.concatenate(
                [jnp.broadcast_to(b8[t:t + 1, :], (8, PAGE)) for t in range(n_tok)], axis=0)
        if valid is not None:
            s = jnp.where(valid, s, NEG)
        m = m_scr[...]
        m_new = jnp.maximum(m, jnp.max(s, axis=1, keepdims=True))
        pr = jnp.exp(s - m_new)
        alpha = jnp.exp(m - m_new)
        l_scr[...] = alpha * l_scr[...] + jnp.sum(pr, axis=1, keepdims=True)
        acc_scr[...] = acc_scr[...] * alpha + _dot(pr.astype(BF16), vpage.astype(BF16))
        m_scr[...] = m_new

    @pl.when(p < n_pages)
    def _():
        update(kp_ref[0], vp_ref[0], p * PAGE + lane, None)

    @pl.when(p == n_pages)
    def _():
        kpos = past_len + lane
        update(kn_ref[0], vn_ref[0], kpos, (lane <= trow) & (lane < n_tok))
        o_ref[0] = acc_scr[...] / l_scr[...]


def _decode(page_table, wq, slope_col, cache_k, cache_v, k_new, v_new, bias, n_tok, past_len):
    b, n_pages = page_table.shape
    rows = n_tok * 8
    width = cache_k.shape[-1]
    last = n_pages - 1
    with_bias = bias is not None
    kern = functools.partial(_decode_kernel, n_pages=n_pages, n_tok=n_tok, past_len=past_len,
                             group=8, with_bias=with_bias)
    page = pl.BlockSpec((1, PAGE, width), lambda i, p, pt: (pt[i, jnp.minimum(p, last)], 0, 0))
    new = pl.BlockSpec((1, PAGE, width), lambda i, p, pt: (i, 0, 0))
    in_specs = [
        pl.BlockSpec((1, rows, width), lambda i, p, pt: (i, 0, 0)),
        pl.BlockSpec((rows, 1), lambda i, p, pt: (0, 0)),
        page, page, new, new,
    ]
    args = [page_table, wq, slope_col, cache_k, cache_v, k_new, v_new]
    if with_bias:
        in_specs.append(pl.BlockSpec((1, 1, 8, PAGE), lambda i, p, pt: (i, p, 0, 0)))
        args.append(bias)
    grid_spec = pltpu.PrefetchScalarGridSpec(
        num_scalar_prefetch=1,
        grid=(b, n_pages + 1),
        in_specs=in_specs,
        out_specs=pl.BlockSpec((1, rows, width), lambda i, p, pt: (i, 0, 0)),
        scratch_shapes=[pltpu.VMEM((rows, width), F32), pltpu.VMEM((rows, 1), F32),
                        pltpu.VMEM((rows, 1), F32)],
    )
    return pl.pallas_call(
        kern,
        grid_spec=grid_spec,
        out_shape=jax.ShapeDtypeStruct((b, rows, width), F32),
        compiler_params=_cparams(("parallel", "arbitrary")),
        name="decode_a" if with_bias else "decode_b",
    )(*args)


def _diffnorm_kernel(o_ref, lq1_ref, lk1_ref, lq2_ref, lk2_ref, g_ref, y_ref, *, lambda_init):
    lam = _lambda_of(lq1_ref, lk1_ref, lq2_ref, lk2_ref, lambda_init)
    hw = 2 * HD_B
    for h in range(H_B):
        o = o_ref[0, :, h * hw:(h + 1) * hw] - lam * o_ref[1, :, h * hw:(h + 1) * hw]
        ms = jnp.mean(o * o, axis=1, keepdims=True)
        y_ref[:, h * hw:(h + 1) * hw] = (o * lax.rsqrt(ms + RMS_EPS) * g_ref[...]
                                         * (1.0 - lambda_init))


def _diffnorm(o12, lq1, lk1, lq2, lk2, g_row, lambda_init):
    n = o12.shape[1]
    return pl.pallas_call(
        functools.partial(_diffnorm_kernel, lambda_init=lambda_init),
        out_shape=jax.ShapeDtypeStruct((n, W_B), F32),
        name="diffnorm",
    )(o12, lq1, lk1, lq2, lk2, g_row)


def _split_columns(w, d_model):
    out, off = {}, 0
    for name, n in SPLITS:
        n = d_model if n is None else n
        out[name] = w[..., off:off + n]
        off += n
    return out


def _pad_cols(a, width):
    return jnp.pad(a, [(0, 0)] * (a.ndim - 1) + [(0, width - a.shape[-1])])


def _proj_weights(w_in, b_in, d_model):
    w = _split_columns(w_in, d_model)
    b = _split_columns(b_in[None, :], d_model)
    order1 = ['q_a', 'k_a', 'v_a', 'q_b', 'k_b', 'v_b', 'q_idx']
    w1 = jnp.concatenate([w[n] for n in order1]
                         + [_pad_cols(w['k_idx'], LANES), _pad_cols(w['w_idx'], LANES)], axis=1)
    b1 = jnp.concatenate([b[n] for n in order1]
                         + [_pad_cols(b['k_idx'], LANES), _pad_cols(b['w_idx'], LANES)], axis=1)
    order2 = ['z_a', 'z_b', 'g_a', 'g_b']
    wz = jnp.concatenate([w[n] for n in order2], axis=1)
    bz = jnp.concatenate([b[n] for n in order2], axis=1)
    return w1.astype(BF16), b1, wz.astype(BF16), bz


def _key_major(v, tq):
    b, t, w = v.shape
    return v.reshape(b, t // tq, tq, w).transpose(0, 1, 3, 2)


def _head_select(o, n_tok, heads, width):
    b = o.shape[0]
    o = o.reshape(b, n_tok, 8, heads, width)
    c = jnp.arange(8)
    own = c * heads // 8
    return jnp.take_along_axis(o, own[None, None, :, None, None], axis=3)[:, :, :, 0, :]


def _prompt_branches(xp, bp, t, w1, b1, lam_args, g, tq, tm, topk, lambda_init):
    (qa, ka32, ka16, va32, va16, qb, kb32, kb16, vb32, vb16, qidx, ki32, wi) = _project(
        xp, w1, b1, tm)
    r3 = lambda a: a.reshape(bp, t, a.shape[-1])
    qit = r3(qidx).reshape(bp, t, H_I, D_I).transpose(0, 2, 3, 1)
    wt = r3(wi).transpose(0, 2, 1)
    o_a = _prompt_a(r3(qa), r3(ka16), _key_major(r3(va16), tq), qit, wt,
                    r3(ki32).astype(BF16), tq, topk)
    o_b = _prompt_b(r3(qb), r3(kb16), _key_major(r3(vb16), tq), *lam_args, g[:, None], tq,
                    lambda_init)
    return o_a.reshape(bp * t, W_A), o_b.reshape(bp * t, W_B), (ka32, va32, ki32, kb32, vb32)


def _sample_branches(xs, bs, ts, page_table, c_k_a, c_v_a, c_k_idx, c_k_b, c_v_b, w1, b1,
                     lam_args, g, topk, lambda_init):
    n_pool = c_k_a.shape[0]
    past_len = page_table.shape[1] * PAGE
    (qa, ka32, _, va32, _, qb, kb32, _, vb32, _, qidx, ki32, wi) = _project(xs, w1, b1, bs * ts)
    s3 = lambda a: a.reshape(bs, ts, a.shape[-1])
    newpad = lambda a: jnp.pad(s3(a), ((0, 0), (0, PAGE - ts), (0, 0)))
    qi_rows = s3(qidx).reshape(bs, ts * H_I, D_I)
    wcol = s3(wi).reshape(bs, ts * H_I, 1)
    bias = _sample_idx(page_table, qi_rows, wcol, c_k_idx, newpad(ki32), ts, topk)
    eye_a = jnp.eye(H_A, dtype=BF16)
    wq_a = (s3(qa).reshape(bs, ts, H_A, 1, HD_A) * eye_a[None, None, :, :, None]
            ).reshape(bs, ts * H_A, W_A)
    slope_a = jnp.tile(jnp.asarray(_slopes(H_A), F32), ts)[:, None]
    oa_rows = _decode(page_table, wq_a, slope_a, c_k_a.reshape(n_pool, PAGE, W_A),
                      c_v_a.reshape(n_pool, PAGE, W_A), newpad(ka32), newpad(va32),
                      bias, ts, past_len)
    o_a = _head_select(oa_rows, ts, H_A, HD_A).reshape(bs * ts, W_A)
    eye_b = jnp.eye(2 * H_B, dtype=BF16)
    wq_b = (s3(qb).reshape(bs, ts, 2 * H_B, 1, HD_B) * eye_b[None, None, :, :, None]
            ).reshape(bs, ts * 2 * H_B, W_B)
    slope_b = jnp.tile(jnp.repeat(jnp.asarray(_slopes(H_B), F32), 2), ts)[:, None]
    ob_rows = _decode(page_table, wq_b, slope_b, c_k_b.reshape(n_pool, PAGE, W_B),
                      c_v_b.reshape(n_pool, PAGE, W_B), newpad(kb32), newpad(vb32),
                      None, ts, past_len)
    ob_sel = _head_select(ob_rows, ts, H_B, 2 * HD_B)
    ob_sel = ob_sel.reshape(bs * ts, H_B, 2, 2 * HD_B).transpose(2, 0, 1, 3)
    o_b = _diffnorm(ob_sel.reshape(2, bs * ts, W_B), *lam_args, g[None, :], lambda_init)
    return o_a, o_b, (ka32, va32, ki32, kb32, vb32)


def kernel(x_prompt, x_sample, cache_k_a, cache_v_a, cache_k_idx, cache_k_b, cache_v_b, page_table,
           w_in, b_in, lambda_q1, lambda_k1, lambda_q2, lambda_k2, subln_g, w_pa, w_pb, w_out,
           ln_g, ln_b):
    depth = w_in.shape[0]
    bp, t, d = x_prompt.shape
    bs, ts, _ = x_sample.shape
    past_len = page_table.shape[1] * PAGE
    alpha = (2.0 * depth) ** 0.25
    tq = min(256, t)
    tm = min(256, bp * t)
    topk_p = min(TOPK_MAX, t // 4)
    topk_s = min(TOPK_MAX, (past_len + ts) // 4)

    xp = x_prompt.reshape(bp * t, d)
    xs = x_sample.reshape(bs * ts, d)
    pk, sk = [], []
    for l in range(depth):
        lambda_init = 0.8 - 0.6 * math.exp(-0.3 * l)
        w1, b1, wz, bz = _proj_weights(w_in[l], b_in[l], d)
        lam_args = (lambda_q1[l][None, :], lambda_k1[l][None, :],
                    lambda_q2[l][None, :], lambda_k2[l][None, :])
        merge_w = (wz, bz, w_pa[l].astype(BF16), w_pb[l].astype(BF16), w_out[l].astype(BF16),
                   ln_g[l][None, :], ln_b[l][None, :])

        o_a, o_b, kv = _prompt_branches(xp, bp, t, w1, b1, lam_args, subln_g[l], tq, tm, topk_p,
                                        lambda_init)
        pk.append(kv)
        xp = _merge(xp, o_a, o_b, *merge_w, tm, alpha)
        o_a, o_b, kv = _sample_branches(xs, bs, ts, page_table, cache_k_a[l], cache_v_a[l],
                                        cache_k_idx[l], cache_k_b[l], cache_v_b[l], w1, b1,
                                        lam_args, subln_g[l], topk_s, lambda_init)
        sk.append(kv)
        xs = _merge(xs, o_a, o_b, *merge_w, bs * ts, alpha)

    def stacked(layers, b, n):
        shapes = ((b, n, H_A, HD_A), (b, n, H_A, HD_A), (b, n, D_I),
                  (b, n, H_B, 2 * HD_B), (b, n, H_B, 2 * HD_B))
        return tuple(jnp.stack([kv[i].reshape(s) for kv in layers]) for i, s in enumerate(shapes))

    return (xp.reshape(bp, t, d), xs.reshape(bs, ts, d)) + stacked(pk, bp, t) + stacked(sk, bs, ts)
```

```python
import functools
import math

import jax
import jax.numpy as jnp
from jax import lax
from jax.experimental import pallas as pl
from jax.experimental.pallas import tpu as pltpu

F32 = jnp.float32
BF16 = jnp.bfloat16
I32 = jnp.int32

H_A, HD_A = 8, 64
H_I, D_I = 8, 32
H_B, HD_B = 4, 64
TOPK_MAX = 256
PAGE = 128
LN_EPS = 1e-5
RMS_EPS = 1e-5
NEG = -1e30
PAD_SCORE = -3e38
LOG2E = math.log2(math.e)

W_A = H_A * HD_A
W_B = H_B * 2 * HD_B
LANES = 128
SUBLANES = 8
VMEM_LIMIT = 56 * 1024 * 1024
TIE_REFINE_STEPS = 16
PAGES_PER_STEP = 8
IDX_PAGES_PER_STEP = 16
CHAINS_PER_STAGE = 2
PART_ROWS = 64

SPLITS = (
    ('q_a', W_A), ('k_a', W_A), ('v_a', W_A),
    ('q_idx', H_I * D_I), ('k_idx', D_I), ('w_idx', H_I), ('z_a', W_A),
    ('q_b', W_B), ('k_b', W_B), ('v_b', W_B), ('z_b', W_B),
    ('g_a', None), ('g_b', None),
)


def _slopes2(n):
    return tuple(LOG2E * 2.0 ** (-8.0 * (i + 1) / n) for i in range(n))


def _cparams(sem):
    return pltpu.CompilerParams(dimension_semantics=sem, vmem_limit_bytes=VMEM_LIMIT)


def _dot_nt(a, b):
    return lax.dot_general(a, b, (((1,), (1,)), ((), ())), preferred_element_type=F32)


def _dot(a, b):
    return jnp.dot(a, b, preferred_element_type=F32)


_C_QA, _C_KA, _C_VA, _C_QB, _C_KB, _C_VB = 0, 512, 1024, 1536, 2048, 2560
_C_QI, _C_KI, _C_WI, _C_END = 3072, 3328, 3456, 3584


def _proj_kernel(x_ref, w_ref, b_ref, qa_ref, ka32_ref, ka16_ref, va32_ref, va16_ref,
                 qb_ref, kb32_ref, kb16_ref, vb32_ref, vb16_ref, qi_ref, ki32_ref, wi_ref):
    x = x_ref[...].astype(BF16)

    def proj(c0, n):
        return _dot(x, w_ref[:, c0:c0 + n]) + b_ref[:, c0:c0 + n]

    qa_ref[...] = (proj(_C_QA, W_A) * (LOG2E * HD_A ** -0.5)).astype(BF16)
    t = proj(_C_KA, W_A)
    ka32_ref[...] = t
    ka16_ref[...] = t.astype(BF16)
    t = proj(_C_VA, W_A)
    va32_ref[...] = t
    va16_ref[...] = t.astype(BF16)
    qb_ref[...] = (proj(_C_QB, W_B) * (LOG2E * HD_B ** -0.5)).astype(BF16)
    t = proj(_C_KB, W_B)
    kb32_ref[...] = t
    kb16_ref[...] = t.astype(BF16)
    t = proj(_C_VB, W_B)
    vb32_ref[...] = t
    vb16_ref[...] = t.astype(BF16)
    qi_ref[...] = proj(_C_QI, H_I * D_I).astype(BF16)
    t = proj(_C_KI, LANES)
    ki32_ref[...] = t[:, :D_I]
    t = proj(_C_WI, LANES)
    wi_ref[...] = t[:, :H_I] * ((H_I ** -0.5) * (D_I ** -0.5))


def _project(x2d, w1, b1, tm):
    n = x2d.shape[0]
    d = x2d.shape[1]
    row = lambda w: pl.BlockSpec((tm, w), lambda i: (i, 0))
    full = lambda a: pl.BlockSpec(a.shape, lambda i: (0, 0))
    sds = jax.ShapeDtypeStruct
    outs = [
        (W_A, BF16), (W_A, F32), (W_A, BF16), (W_A, F32), (W_A, BF16),
        (W_B, BF16), (W_B, F32), (W_B, BF16), (W_B, F32), (W_B, BF16),
        (H_I * D_I, BF16), (D_I, F32), (H_I, F32),
    ]
    return pl.pallas_call(
        _proj_kernel,
        grid=(n // tm,),
        in_specs=[pl.BlockSpec((tm, d), lambda i: (i, 0)), full(w1), full(b1)],
        out_specs=[row(w) for w, _ in outs],
        out_shape=[sds((n, w), dt) for w, dt in outs],
        compiler_params=_cparams(("parallel",)),
        name="proj",
    )(x2d, w1, b1)


def _float_of_key(u):
    s = u ^ jnp.int32(-2 ** 31)
    fb = jnp.where(s >= 0, s, s ^ jnp.int32(0x7FFFFFFF))
    return lax.bitcast_convert_type(fb, F32)


def _kth_largest_cell(count_ge, shape, k):
    def bit_body(i, prefix):
        bit = lax.shift_left(jnp.int32(1), jnp.int32(31) - i)
        cand = prefix | bit
        return jnp.where(count_ge(_float_of_key(cand)) >= k, cand, prefix)

    prefix = lax.fori_loop(0, 32, bit_body, jnp.zeros(shape, I32))

    def refine(_, c):
        lo, hi = c
        mid = lo + (hi - lo) * 0.5
        ok = count_ge(mid) >= k
        return jnp.where(ok, mid, lo), jnp.where(ok, hi, mid)

    return lax.fori_loop(0, TIE_REFINE_STEPS, refine,
                         (_float_of_key(prefix), _float_of_key(prefix + 1)))


def _fold_rows(x, op):
    out = x[:PART_ROWS]
    for i in range(1, x.shape[0] // PART_ROWS):
        out = op(out, x[i * PART_ROWS:(i + 1) * PART_ROWS])
    return out


def _chain_pipeline(n_chain, nkb, tq, acc_rows, scores, offset, values, s_scr, p_scr):
    g = CHAINS_PER_STAGE
    n_group = n_chain // g
    neg0 = jnp.full((PART_ROWS, tq), NEG, F32)
    zero = jnp.zeros((PART_ROWS, tq), F32)
    zacc = jnp.zeros((acc_rows, tq), F32)
    ms, ls, accs = [None] * n_chain, [None] * n_chain, [None] * n_chain
    for grp in range(n_group + 2):
        def body(kb, carry, grp=grp):
            m_a, l_b, acc_c = list(carry[0]), list(carry[1]), list(carry[2])
            for i in range(g):
                if grp >= 2:
                    c = (grp - 2) * g + i
                    acc_c[i] = acc_c[i] + _dot(values(c, kb), p_scr[i, kb])
                if 1 <= grp <= n_group:
                    c = (grp - 1) * g + i
                    p = jnp.exp2(s_scr[i, kb] - (ms[c] - offset(c, kb)))
                    p_scr[i, kb] = p.astype(BF16)
                    l_b[i] = l_b[i] + _fold_rows(p, jnp.add)
                if grp < n_group:
                    c = grp * g + i
                    s = scores(c, kb)
                    s_scr[i, kb] = s
                    m_a[i] = jnp.maximum(m_a[i], _fold_rows(s, jnp.maximum) + offset(c, kb))
            return tuple(m_a), tuple(l_b), tuple(acc_c)

        m_a, l_b, acc_c = lax.fori_loop(0, nkb, body, ((neg0,) * g, (zero,) * g, (zacc,) * g))
        for i in range(g):
            if grp < n_group:
                ms[grp * g + i] = jnp.max(m_a[i], axis=0, keepdims=True)
            if 1 <= grp <= n_group:
                ls[(grp - 1) * g + i] = jnp.sum(l_b[i], axis=0, keepdims=True)
            if grp >= 2:
                accs[(grp - 2) * g + i] = acc_c[i]
    return ls, accs


def _block_geometry(qi, tq, tk):
    nkb = ((qi + 1) * tq + tk - 1) // tk
    kl = lax.broadcasted_iota(I32, (tk, tq), 0)
    ql = lax.broadcasted_iota(I32, (tk, tq), 1)

    def koff(kb):
        return kb * tk - qi * tq + jnp.zeros((1, tq), I32)

    return nkb, kl, ql, koff


def _prompt_a_kernel(q_ref, k_ref, vt_ref, qit_ref, wt_ref, ki_ref, o_ref,
                     sc_scr, s_scr, p_scr, ab_scr, ot_scr, *, tq, tk, topk):
    qi = pl.program_id(1)
    nkb, kl, ql, koff = _block_geometry(qi, tq, tk)
    last = nkb - 1
    causal_last = (kl + koff(last)) <= ql
    wt = wt_ref[0]
    k = float(topk)

    def idx_body(kb, c):
        ki = ki_ref[0, pl.ds(pl.multiple_of(kb * tk, tk), tk), :]
        acc = jnp.zeros((tk, tq), F32)
        for h in range(H_I):
            r = _dot(ki, qit_ref[0, h])
            acc = acc + jnp.maximum(r, 0.0) * wt[h:h + 1, :]
        sc_scr[kb] = acc
        return c

    lax.fori_loop(0, nkb, idx_body, 0)
    sc_scr[last] = jnp.where(causal_last, sc_scr[last], NEG)

    def count_ge(v):
        def body(kb, cnt):
            return cnt + _fold_rows(jnp.where(sc_scr[kb] >= v, 1.0, 0.0), jnp.add)
        part = lax.fori_loop(0, nkb, body, jnp.zeros((PART_ROWS, tq), F32))
        return jnp.sum(part, axis=0, keepdims=True)

    def bit_body(i, prefix):
        bit = lax.shift_left(jnp.int32(1), jnp.int32(31) - i)
        cand = prefix | bit
        return jnp.where(count_ge(_float_of_key(cand)) >= k, cand, prefix)

    prefix = lax.fori_loop(0, 32, bit_body, jnp.zeros((1, tq), I32))
    thr = _float_of_key(prefix)
    tie = jnp.max(count_ge(thr)) > k

    @pl.when(jnp.logical_not(tie))
    def _():
        def sel_body(kb, c):
            sc_scr[kb] = jnp.where(sc_scr[kb] >= thr, 0.0, NEG)
            return c
        lax.fori_loop(0, nkb, sel_body, 0)

    @pl.when(tie)
    def _():
        def refine(_, c):
            lo, hi = c
            mid = lo + (hi - lo) * 0.5
            ok = count_ge(mid) >= k
            return jnp.where(ok, mid, lo), jnp.where(ok, hi, mid)

        lo, hi = lax.fori_loop(0, TIE_REFINE_STEPS, refine, (thr, _float_of_key(prefix + 1)))
        need = k - count_ge(hi)
        below = (lax.broadcasted_iota(I32, (tk, tk), 1)
                 < lax.broadcasted_iota(I32, (tk, tk), 0)).astype(BF16)

        def sel_body(kb, seen):
            t = sc_scr[kb]
            eq = (t >= lo) & (t < hi)
            eqf = jnp.where(eq, 1.0, 0.0)
            rank = _dot(below, eqf.astype(BF16)) + seen
            sc_scr[kb] = jnp.where((t >= hi) | (eq & (rank < need)), 0.0, NEG)
            return seen + jnp.sum(eqf, axis=0, keepdims=True)

        lax.fori_loop(0, nkb, sel_body, jnp.zeros((1, tq), F32))

    sc_scr[last] = jnp.where(causal_last, sc_scr[last], NEG)

    slopes = _slopes2(H_A)
    klf = kl.astype(F32)
    lane = lax.broadcasted_iota(I32, (tq, LANES), 1)
    qms = []
    for h in range(H_A):
        pair, odd = h // 2, h % 2
        qp = q_ref[0, :, pair * LANES:(pair + 1) * LANES]
        qms.append(jnp.where((lane >= HD_A) == bool(odd), qp, jnp.zeros_like(qp)))
        ab_scr[h] = slopes[h] * klf

    def scores(h, kb):
        cols = slice((h // 2) * LANES, (h // 2 + 1) * LANES)
        kblk = k_ref[0, pl.ds(pl.multiple_of(kb * tk, tk), tk), cols]
        return _dot_nt(kblk, qms[h]) + (sc_scr[kb] + ab_scr[h])

    def offset(h, kb):
        return slopes[h] * koff(kb).astype(F32)

    def values(h, kb):
        return vt_ref[0, kb, h * HD_A:(h + 1) * HD_A, :]

    ls, accs = _chain_pipeline(H_A, nkb, tq, HD_A, scores, offset, values, s_scr, p_scr)
    for h in range(H_A):
        ot_scr[h * HD_A:(h + 1) * HD_A, :] = accs[h] / ls[h]
    o_ref[0] = ot_scr[...].T.astype(o_ref.dtype)


def _prompt_a(qa, ka, vta, qit, wt, ki, tq, tk, topk):
    b, t, _ = qa.shape
    nq, nk = t // tq, t // tk
    kern = functools.partial(_prompt_a_kernel, tq=tq, tk=tk, topk=topk)
    return pl.pallas_call(
        kern,
        grid=(b, nq),
        in_specs=[
            pl.BlockSpec((1, tq, W_A), lambda i, j: (i, j, 0)),
            pl.BlockSpec((1, t, W_A), lambda i, j: (i, 0, 0)),
            pl.BlockSpec((1, nk, W_A, tk), lambda i, j: (i, 0, 0, 0)),
            pl.BlockSpec((1, H_I, D_I, tq), lambda i, j: (i, 0, 0, j)),
            pl.BlockSpec((1, H_I, tq), lambda i, j: (i, 0, j)),
            pl.BlockSpec((1, t, D_I), lambda i, j: (i, 0, 0)),
        ],
        out_specs=pl.BlockSpec((1, tq, W_A), lambda i, j: (i, j, 0)),
        out_shape=jax.ShapeDtypeStruct((b, t, W_A), F32),
        scratch_shapes=[
            pltpu.VMEM((nk, tk, tq), F32),
            pltpu.VMEM((CHAINS_PER_STAGE, nk, tk, tq), F32),
            pltpu.VMEM((CHAINS_PER_STAGE, nk, tk, tq), BF16),
            pltpu.VMEM((H_A, tk, tq), F32),
            pltpu.VMEM((W_A, tq), F32),
        ],
        compiler_params=_cparams(("parallel", "arbitrary")),
        name="prompt_a",
    )(qa, ka, vta, qit, wt, ki)


def _lambda_of(lq1_ref, lk1_ref, lq2_ref, lk2_ref, lambda_init):
    s1 = jnp.sum(lq1_ref[...] * lk1_ref[...], axis=1, keepdims=True)
    s2 = jnp.sum(lq2_ref[...] * lk2_ref[...], axis=1, keepdims=True)
    return jnp.exp(s1) - jnp.exp(s2) + lambda_init


def _prompt_b_kernel(q_ref, k_ref, vt_ref, lq1_ref, lk1_ref, lq2_ref, lk2_ref, g_ref, o_ref,
                     s_scr, p_scr, ab_scr, ot_scr, *, tq, tk, lambda_init):
    qi = pl.program_id(1)
    nkb, kl, ql, koff = _block_geometry(qi, tq, tk)
    last = nkb - 1
    causal_last = (kl + koff(last)) <= ql
    klf = kl.astype(F32)
    lane = lax.broadcasted_iota(I32, (tq, LANES), 1)
    lam = _lambda_of(lq1_ref, lk1_ref, lq2_ref, lk2_ref, lambda_init)
    slopes = _slopes2(H_B)
    qms = []
    for h in range(H_B):
        qp = q_ref[0, :, h * LANES:(h + 1) * LANES]
        qms.append(jnp.where(lane < HD_B, qp, jnp.zeros_like(qp)))
        qms.append(jnp.where(lane >= HD_B, qp, jnp.zeros_like(qp)))
        a_h = slopes[h] * klf
        ab_scr[0, h] = a_h
        ab_scr[1, h] = jnp.where(causal_last, a_h, NEG)

    def scores(c, kb):
        h = c // 2
        kblk = k_ref[0, pl.ds(pl.multiple_of(kb * tk, tk), tk), h * LANES:(h + 1) * LANES]
        return _dot_nt(kblk, qms[c]) + ab_scr[(kb == last).astype(I32), h]

    def offset(c, kb):
        return slopes[c // 2] * koff(kb).astype(F32)

    def values(c, kb):
        h = c // 2
        return vt_ref[0, kb, h * LANES:(h + 1) * LANES, :]

    ls, accs = _chain_pipeline(2 * H_B, nkb, tq, LANES, scores, offset, values, s_scr, p_scr)
    g = g_ref[...]
    for h in range(H_B):
        o = accs[2 * h] / ls[2 * h] - lam * (accs[2 * h + 1] / ls[2 * h + 1])
        ms = jnp.mean(o * o, axis=0, keepdims=True)
        o = o * lax.rsqrt(ms + RMS_EPS) * g
        ot_scr[h * LANES:(h + 1) * LANES, :] = o * (1.0 - lambda_init)
    o_ref[0] = ot_scr[...].T.astype(o_ref.dtype)


def _prompt_b(qb, kb, vtb, lq1, lk1, lq2, lk2, g_col, tq, tk, lambda_init):
    b, t, _ = qb.shape
    nq, nk = t // tq, t // tk
    kern = functools.partial(_prompt_b_kernel, tq=tq, tk=tk, lambda_init=lambda_init)
    small = lambda a: pl.BlockSpec(a.shape, lambda i, j: (0, 0))
    return pl.pallas_call(
        kern,
        grid=(b, nq),
        in_specs=[
            pl.BlockSpec((1, tq, W_B), lambda i, j: (i, j, 0)),
            pl.BlockSpec((1, t, W_B), lambda i, j: (i, 0, 0)),
            pl.BlockSpec((1, nk, W_B, tk), lambda i, j: (i, 0, 0, 0)),
            small(lq1), small(lk1), small(lq2), small(lk2), small(g_col),
        ],
        out_specs=pl.BlockSpec((1, tq, W_B), lambda i, j: (i, j, 0)),
        out_shape=jax.ShapeDtypeStruct((b, t, W_B), F32),
        scratch_shapes=[
            pltpu.VMEM((CHAINS_PER_STAGE, nk, tk, tq), F32),
            pltpu.VMEM((CHAINS_PER_STAGE, nk, tk, tq), BF16),
            pltpu.VMEM((2, H_B, tk, tq), F32),
            pltpu.VMEM((W_B, tq), F32),
        ],
        compiler_params=_cparams(("parallel", "arbitrary")),
        name="prompt_b",
    )(qb, kb, vtb, lq1, lk1, lq2, lk2, g_col)


def _sigmoid(x):
    return 1.0 / (1.0 + jnp.exp(-x))


def _merge_kernel(x_ref, oa_ref, ob_ref, wz_ref, bz_ref, wpa_ref, wpb_ref, wout_ref,
                  lng_ref, lnb_ref, y_ref, *, alpha):
    x = x_ref[...]
    xh = x.astype(BF16)

    def proj(c0, n):
        return _dot(xh, wz_ref[:, c0:c0 + n]) + bz_ref[:, c0:c0 + n]

    d = x.shape[1]
    za = proj(0, W_A)
    ua = (oa_ref[...] * (za * _sigmoid(za))).astype(BF16)
    ya = _dot(ua, wpa_ref[...])
    zb = proj(W_A, W_B)
    ub = (ob_ref[...] * (zb * _sigmoid(zb))).astype(BF16)
    yb = _dot(ub, wpb_ref[...])
    m = _sigmoid(proj(W_A + W_B, d)) * ya + _sigmoid(proj(W_A + W_B + d, d)) * yb
    r = alpha * x + _dot(m.astype(BF16), wout_ref[...])
    mu = jnp.mean(r, axis=1, keepdims=True)
    rc = r - mu
    var = jnp.mean(rc * rc, axis=1, keepdims=True)
    y_ref[...] = rc * lax.rsqrt(var + LN_EPS) * lng_ref[...] + lnb_ref[...]


def _merge(x2d, oa, ob, wz, bz, wpa, wpb, wout, lng, lnb, tm, alpha):
    n, d = x2d.shape
    row = lambda w: pl.BlockSpec((tm, w), lambda i: (i, 0))
    full = lambda a: pl.BlockSpec(a.shape, lambda i: (0, 0))
    return pl.pallas_call(
        functools.partial(_merge_kernel, alpha=alpha),
        grid=(n // tm,),
        in_specs=[row(d), row(W_A), row(W_B), full(wz), full(bz), full(wpa), full(wpb),
                  full(wout), full(lng), full(lnb)],
        out_specs=row(d),
        out_shape=jax.ShapeDtypeStruct((n, d), F32),
        compiler_params=_cparams(("parallel",)),
        name="merge",
    )(x2d, oa, ob, wz, bz, wpa, wpb, wout, lng, lnb)


def _page_specs(n, shape, first_page):
    def spec(j):
        return pl.BlockSpec((1,) + shape, lambda i, s, pt, j=j: (pt[i, first_page(s) + j], 0, 0))
    return [spec(j) for j in range(n)]


def _sample_idx_kernel(pt_ref, qi_ref, wcol_ref, kin_ref, *rest, n_pages, pps, n_tok, topk):
    kip_refs, (bias_ref, sc_scr) = rest[:pps], rest[pps:]
    step = pl.program_id(1)
    n_steps = n_pages // pps
    lane = lax.broadcasted_iota(I32, (SUBLANES, PAGE), 1)
    trow = lax.broadcasted_iota(I32, (SUBLANES, PAGE), 0)
    k = float(topk)

    def scores(kpage_t):
        r = _dot(qi_ref[0], kpage_t.astype(BF16))
        r = jnp.maximum(r, 0.0) * wcol_ref[0]
        per_tok = [jnp.sum(r[t * H_I:(t + 1) * H_I, :], axis=0, keepdims=True)
                   for t in range(n_tok)]
        pad = [jnp.zeros((1, PAGE), F32)] * (SUBLANES - n_tok)
        return jnp.concatenate(per_tok + pad, axis=0)

    for j in range(pps):
        sc_scr[step * pps + j] = scores(kip_refs[j][0])

    @pl.when(step == n_steps - 1)
    def _():
        new_ok = (lane <= trow) & (lane < n_tok)
        s = jnp.where(lane <= trow, scores(kin_ref[0]), NEG)
        sc_scr[n_pages] = jnp.where(lane < n_tok, s, PAD_SCORE)

        def count_ge(v):
            parts = [jnp.zeros((SUBLANES, PAGE), F32)] * 4
            for j in range(n_pages + 1):
                parts[j % 4] = parts[j % 4] + jnp.where(sc_scr[j] >= v, 1.0, 0.0)
            return jnp.sum((parts[0] + parts[1]) + (parts[2] + parts[3]), axis=1, keepdims=True)

        lo, hi = _kth_largest_cell(count_ge, (SUBLANES, 1), k)
        need = k - count_ge(hi)
        before = (lax.broadcasted_iota(I32, (PAGE, PAGE), 0)
                  < lax.broadcasted_iota(I32, (PAGE, PAGE), 1)).astype(BF16)
        seen = jnp.zeros((SUBLANES, 1), F32)
        for j in range(n_pages + 1):
            t = sc_scr[j]
            eq = (t >= lo) & (t < hi)
            eqf = jnp.where(eq, 1.0, 0.0)
            rank = _dot(eqf.astype(BF16), before) + seen
            sel = (t >= hi) | (eq & (rank < need))
            if j == n_pages:
                sel = sel & new_ok
            bias_ref[0, j] = jnp.where(sel, 0.0, NEG)
            seen = seen + jnp.sum(eqf, axis=1, keepdims=True)


def _sample_idx(page_table, qi_rows, wcol, cache_ki_t, ki_new_t, n_tok, topk):
    b, n_pages = page_table.shape
    rows = n_tok * H_I
    pps = min(IDX_PAGES_PER_STEP, n_pages)
    kern = functools.partial(_sample_idx_kernel, n_pages=n_pages, pps=pps, n_tok=n_tok,
                             topk=topk)
    grid_spec = pltpu.PrefetchScalarGridSpec(
        num_scalar_prefetch=1,
        grid=(b, n_pages // pps),
        in_specs=[
            pl.BlockSpec((1, rows, D_I), lambda i, s, pt: (i, 0, 0)),
            pl.BlockSpec((1, rows, 1), lambda i, s, pt: (i, 0, 0)),
            pl.BlockSpec((1, D_I, PAGE), lambda i, s, pt: (i, 0, 0)),
        ] + _page_specs(pps, (D_I, PAGE), lambda s: s * pps),
        out_specs=pl.BlockSpec((1, n_pages + 1, SUBLANES, PAGE), lambda i, s, pt: (i, 0, 0, 0)),
        scratch_shapes=[pltpu.VMEM((n_pages + 1, SUBLANES, PAGE), F32)],
    )
    return pl.pallas_call(
        kern,
        grid_spec=grid_spec,
        out_shape=jax.ShapeDtypeStruct((b, n_pages + 1, SUBLANES, PAGE), F32),
        compiler_params=_cparams(("parallel", "arbitrary")),
        name="sample_idx",
    )(page_table, qi_rows, wcol, ki_new_t, *([cache_ki_t] * pps))


def _softmax_update(s, m_scr, l_scr):
    m = m_scr[...]
    m_new = jnp.maximum(m, jnp.max(s, axis=1, keepdims=True))
    p = jnp.exp2(s - m_new)
    alpha = jnp.exp2(m - m_new)
    l_scr[...] = alpha * l_scr[...] + jnp.sum(p, axis=1, keepdims=True)
    m_scr[...] = m_new
    return p, alpha


def _decode_a_kernel(pt_ref, wq_ref, slope_ref, kn_ref, vn_ref, bias_ref, biasn_ref, *rest,
                     n_pages, pps, n_tok, past_len):
    kp_refs, vp_refs = rest[:pps], rest[pps:2 * pps]
    o_ref, acc_scr, m_scr, l_scr = rest[2 * pps:]
    step = pl.program_id(1)
    n_steps = n_pages // pps
    rows = n_tok * H_A

    @pl.when(step == 0)
    def _():
        acc_scr[...] = jnp.zeros_like(acc_scr)
        m_scr[...] = jnp.full_like(m_scr, NEG)
        l_scr[...] = jnp.zeros_like(l_scr)

    def update(k_ts, v_ts, first_pos, b8s, new_keys):
        n = len(k_ts)
        lane = lax.broadcasted_iota(I32, (rows, n * PAGE), 1)
        trow = lax.broadcasted_iota(I32, (rows, n * PAGE), 0) // H_A
        s = jnp.concatenate([_dot(wq_ref[0], k_t.astype(BF16)) for k_t in k_ts], axis=1)
        s = s - slope_ref[...] * (past_len + trow - (first_pos + lane)).astype(F32)
        s = s + jnp.concatenate(
            [jnp.broadcast_to(jnp.concatenate([b8[t:t + 1, :] for b8 in b8s], axis=1),
                              (H_A, n * PAGE)) for t in range(n_tok)], axis=0)
        if new_keys:
            s = jnp.where((lane <= trow) & (lane < n_tok), s, NEG)
        p, alpha = _softmax_update(s, m_scr, l_scr)
        p = p.astype(BF16)
        pv = _dot_nt(p[:, :PAGE], v_ts[0].astype(BF16))
        for j in range(1, n):
            pv = pv + _dot_nt(p[:, j * PAGE:(j + 1) * PAGE], v_ts[j].astype(BF16))
        acc_scr[...] = acc_scr[...] * alpha + pv

    update([r[0] for r in kp_refs], [r[0] for r in vp_refs], step * (pps * PAGE),
           [bias_ref[0, j] for j in range(pps)], False)

    @pl.when(step == n_steps - 1)
    def _():
        update([kn_ref[0]], [vn_ref[0]], past_len, [biasn_ref[0, 0]], True)
        o = acc_scr[...] / l_scr[...]
        own = (lax.broadcasted_iota(I32, (rows, W_A), 1) // HD_A
               == lax.broadcasted_iota(I32, (rows, W_A), 0) % H_A)
        o = jnp.where(own, o, 0.0)
        o_ref[0] = jnp.concatenate(
            [jnp.sum(o[t * H_A:(t + 1) * H_A, :], axis=0, keepdims=True) for t in range(n_tok)],
            axis=0)


def _decode_a(page_table, wq, slope_col, cache_k_t, cache_v_t, k_new_t, v_new_t, bias, n_tok,
              past_len):
    b, n_pages = page_table.shape
    rows = n_tok * H_A
    pps = min(PAGES_PER_STEP, n_pages)
    kern = functools.partial(_decode_a_kernel, n_pages=n_pages, pps=pps, n_tok=n_tok,
                             past_len=past_len)
    pages = _page_specs(pps, (W_A, PAGE), lambda s: s * pps)
    per_batch = lambda shape: pl.BlockSpec((1,) + shape, lambda i, s, pt: (i,) + (0,) * len(shape))
    grid_spec = pltpu.PrefetchScalarGridSpec(
        num_scalar_prefetch=1,
        grid=(b, n_pages // pps),
        in_specs=[
            per_batch((rows, W_A)),
            pl.BlockSpec((rows, 1), lambda i, s, pt: (0, 0)),
            per_batch((W_A, PAGE)), per_batch((W_A, PAGE)),
            pl.BlockSpec((1, pps, SUBLANES, PAGE), lambda i, s, pt: (i, s, 0, 0)),
            pl.BlockSpec((1, 1, SUBLANES, PAGE), lambda i, s, pt: (i, n_pages, 0, 0)),
        ] + pages + pages,
        out_specs=per_batch((n_tok, W_A)),
        scratch_shapes=[pltpu.VMEM((rows, W_A), F32), pltpu.VMEM((rows, 1), F32),
                        pltpu.VMEM((rows, 1), F32)],
    )
    return pl.pallas_call(
        kern,
        grid_spec=grid_spec,
        out_shape=jax.ShapeDtypeStruct((b, n_tok, W_A), F32),
        compiler_params=_cparams(("parallel", "arbitrary")),
        name="decode_a",
    )(page_table, wq, slope_col, k_new_t, v_new_t, bias, bias,
      *([cache_k_t] * pps), *([cache_v_t] * pps))


def _decode_b_kernel(pt_ref, wq_ref, kn_ref, vn_ref, lq1_ref, lk1_ref, lq2_ref, lk2_ref, g_ref,
                     *rest, n_pages, pps, n_tok, past_len, lambda_init):
    kp_refs, vp_refs = rest[:pps], rest[pps:2 * pps]
    o_ref, acc_scr, m_scr, l_scr = rest[2 * pps:]
    step = pl.program_id(1)
    n_steps = n_pages // pps
    hrows = 2 * n_tok
    rows = H_B * hrows
    slope_col = jnp.concatenate([jnp.full((hrows, 1), s, F32) for s in _slopes2(H_B)], axis=0)

    @pl.when(step == 0)
    def _():
        acc_scr[...] = jnp.zeros_like(acc_scr)
        m_scr[...] = jnp.full_like(m_scr, NEG)
        l_scr[...] = jnp.zeros_like(l_scr)

    def update(k_hs, v_hs, first_pos, new_keys):
        n = len(k_hs[0])
        lane = lax.broadcasted_iota(I32, (rows, n * PAGE), 1)
        trow = lax.broadcasted_iota(I32, (rows, n * PAGE), 0) % n_tok
        s = jnp.concatenate(
            [jnp.concatenate([_dot_nt(wq_ref[0, h], k.astype(BF16)) for k in k_hs[h]], axis=1)
             for h in range(H_B)], axis=0)
        s = s - slope_col * (past_len + trow - (first_pos + lane)).astype(F32)
        if new_keys:
            s = jnp.where((lane <= trow) & (lane < n_tok), s, NEG)
        p, alpha = _softmax_update(s, m_scr, l_scr)
        p = p.astype(BF16)
        pvs = []
        for h in range(H_B):
            ph = p[h * hrows:(h + 1) * hrows, :]
            pv = _dot(ph[:, :PAGE], v_hs[h][0].astype(BF16))
            for j in range(1, n):
                pv = pv + _dot(ph[:, j * PAGE:(j + 1) * PAGE], v_hs[h][j].astype(BF16))
            pvs.append(pv)
        acc_scr[...] = acc_scr[...] * alpha + jnp.concatenate(pvs, axis=0)

    head_rows = lambda refs, h: [r[0, pl.ds(h, PAGE, stride=H_B), :] for r in refs]
    update([head_rows(kp_refs, h) for h in range(H_B)],
           [head_rows(vp_refs, h) for h in range(H_B)], step * (pps * PAGE), False)

    @pl.when(step == n_steps - 1)
    def _():
        update([[kn_ref[0, h]] for h in range(H_B)], [[vn_ref[0, h]] for h in range(H_B)],
               past_len, True)
        lam = _lambda_of(lq1_ref, lk1_ref, lq2_ref, lk2_ref, lambda_init)
        o_all = acc_scr[...] / l_scr[...]
        for h in range(H_B):
            o = (o_all[h * hrows:h * hrows + n_tok, :]
                 - lam * o_all[h * hrows + n_tok:(h + 1) * hrows, :])
            ms = jnp.mean(o * o, axis=1, keepdims=True)
            o = o * lax.rsqrt(ms + RMS_EPS) * g_ref[...] * (1.0 - lambda_init)
            o_ref[0, :, h * LANES:(h + 1) * LANES] = o


def _decode_b(page_table, wq, cache_k_r, cache_v_r, k_new, v_new, lam_args, g_row, n_tok,
              past_len, lambda_init):
    b, n_pages = page_table.shape
    rows = 2 * n_tok
    pps = min(PAGES_PER_STEP, n_pages)
    kern = functools.partial(_decode_b_kernel, n_pages=n_pages, pps=pps, n_tok=n_tok,
                             past_len=past_len, lambda_init=lambda_init)
    pages = _page_specs(pps, (PAGE * H_B, 2 * HD_B), lambda s: s * pps)
    per_batch = lambda shape: pl.BlockSpec((1,) + shape, lambda i, s, pt: (i,) + (0,) * len(shape))
    small = lambda a: pl.BlockSpec(a.shape, lambda i, s, pt: (0, 0))
    grid_spec = pltpu.PrefetchScalarGridSpec(
        num_scalar_prefetch=1,
        grid=(b, n_pages // pps),
        in_specs=[
            per_batch((H_B, rows, 2 * HD_B)),
            per_batch((H_B, PAGE, 2 * HD_B)), per_batch((H_B, PAGE, 2 * HD_B)),
        ] + [small(a) for a in lam_args] + [small(g_row)] + pages + pages,
        out_specs=per_batch((n_tok, W_B)),
        scratch_shapes=[pltpu.VMEM((H_B * rows, 2 * HD_B), F32), pltpu.VMEM((H_B * rows, 1), F32),
                        pltpu.VMEM((H_B * rows, 1), F32)],
    )
    return pl.pallas_call(
        kern,
        grid_spec=grid_spec,
        out_shape=jax.ShapeDtypeStruct((b, n_tok, W_B), F32),
        compiler_params=_cparams(("parallel", "arbitrary")),
        name="decode_b",
    )(page_table, wq, k_new, v_new, *lam_args, g_row,
      *([cache_k_r] * pps), *([cache_v_r] * pps))


def _split_columns(w, d_model):
    out, off = {}, 0
    for name, n in SPLITS:
        n = d_model if n is None else n
        out[name] = w[..., off:off + n]
        off += n
    return out


def _pad_cols(a, width):
    return jnp.pad(a, [(0, 0)] * (a.ndim - 1) + [(0, width - a.shape[-1])])


def _proj_weights(w_in, b_in, d_model):
    w = _split_columns(w_in, d_model)
    b = _split_columns(b_in[None, :], d_model)
    order1 = ['q_a', 'k_a', 'v_a', 'q_b', 'k_b', 'v_b', 'q_idx']
    w1 = jnp.concatenate([w[n] for n in order1]
                         + [_pad_cols(w['k_idx'], LANES), _pad_cols(w['w_idx'], LANES)], axis=1)
    b1 = jnp.concatenate([b[n] for n in order1]
                         + [_pad_cols(b['k_idx'], LANES), _pad_cols(b['w_idx'], LANES)], axis=1)
    order2 = ['z_a', 'z_b', 'g_a', 'g_b']
    wz = jnp.concatenate([w[n] for n in order2], axis=1)
    bz = jnp.concatenate([b[n] for n in order2], axis=1)
    return w1.astype(BF16), b1, wz.astype(BF16), bz


def _key_major(v, tk):
    b, t, w = v.shape
    return v.reshape(b, t // tk, tk, w).transpose(0, 1, 3, 2)


def _prompt_branches(xp, bp, t, w1, b1, lam_args, g, tq, tk, tm, topk, lambda_init):
    (qa, ka32, ka16, va32, va16, qb, kb32, kb16, vb32, vb16, qidx, ki32, wi) = _project(
        xp, w1, b1, tm)
    r3 = lambda a: a.reshape(bp, t, a.shape[-1])
    qit = r3(qidx).reshape(bp, t, H_I, D_I).transpose(0, 2, 3, 1)
    wt = r3(wi).transpose(0, 2, 1)
    o_a = _prompt_a(r3(qa), r3(ka16), _key_major(r3(va16), tk), qit, wt,
                    r3(ki32).astype(BF16), tq, tk, topk)
    o_b = _prompt_b(r3(qb), r3(kb16), _key_major(r3(vb16), tk), *lam_args, g[:, None], tq, tk,
                    lambda_init)
    return o_a.reshape(bp * t, W_A), o_b.reshape(bp * t, W_B), (ka32, va32, ki32, kb32, vb32)


def _sample_branches(xs, bs, ts, page_table, c_k_a, c_v_a, c_k_idx, c_k_b, c_v_b, w1, b1,
                     lam_args, g, topk, lambda_init):
    n_pool = c_k_a.shape[0]
    past_len = page_table.shape[1] * PAGE
    (qa, ka32, _, va32, _, qb, kb32, _, vb32, _, qidx, ki32, wi) = _project(xs, w1, b1, bs * ts)
    s3 = lambda a: a.reshape(bs, ts, a.shape[-1])
    new_t = lambda a: jnp.pad(s3(a).transpose(0, 2, 1), ((0, 0), (0, 0), (0, PAGE - ts)))
    qi_rows = s3(qidx).reshape(bs, ts * H_I, D_I)
    wcol = s3(wi).reshape(bs, ts * H_I, 1)
    bias = _sample_idx(page_table, qi_rows, wcol, c_k_idx.transpose(0, 2, 1), new_t(ki32), ts,
                       topk)
    eye_a = jnp.eye(H_A, dtype=BF16)
    wq_a = (s3(qa).reshape(bs, ts, H_A, 1, HD_A) * eye_a[None, None, :, :, None]
            ).reshape(bs, ts * H_A, W_A)
    slope_a = jnp.tile(jnp.asarray(_slopes2(H_A), F32), ts)[:, None]
    page_t = lambda c: c.transpose(0, 2, 3, 1).reshape(n_pool, W_A, PAGE)
    o_a = _decode_a(page_table, wq_a, slope_a, page_t(c_k_a), page_t(c_v_a), new_t(ka32),
                    new_t(va32), bias, ts, past_len)
    qb4 = s3(qb).reshape(bs, ts, H_B, 2 * HD_B).transpose(0, 2, 1, 3)
    first = (jnp.arange(2 * HD_B) < HD_B)
    wq_b = jnp.concatenate([jnp.where(first, qb4, 0), jnp.where(first, 0, qb4)], axis=2)
    new_h = lambda a: jnp.pad(s3(a).reshape(bs, ts, H_B, 2 * HD_B).transpose(0, 2, 1, 3),
                              ((0, 0), (0, 0), (0, PAGE - ts), (0, 0)))
    page_r = lambda c: c.reshape(n_pool, PAGE * H_B, 2 * HD_B)
    o_b = _decode_b(page_table, wq_b, page_r(c_k_b), page_r(c_v_b), new_h(kb32), new_h(vb32),
                    lam_args, g[None, :], ts, past_len, lambda_init)
    return (o_a.reshape(bs * ts, W_A), o_b.reshape(bs * ts, W_B),
            (ka32, va32, ki32, kb32, vb32))


def kernel(x_prompt, x_sample, cache_k_a, cache_v_a, cache_k_idx, cache_k_b, cache_v_b, page_table,
           w_in, b_in, lambda_q1, lambda_k1, lambda_q2, lambda_k2, subln_g, w_pa, w_pb, w_out,
           ln_g, ln_b):
    depth = w_in.shape[0]
    bp, t, d = x_prompt.shape
    bs, ts, _ = x_sample.shape
    past_len = page_table.shape[1] * PAGE
    alpha = (2.0 * depth) ** 0.25
    tq = min(256, t)
    tk = min(512, t)
    tm = min(256, bp * t)
    topk_p = min(TOPK_MAX, t // 4)
    topk_s = min(TOPK_MAX, (past_len + ts) // 4)

    xp = x_prompt.reshape(bp * t, d)
    xs = x_sample.reshape(bs * ts, d)
    pk, sk = [], []
    for l in range(depth):
        lambda_init = 0.8 - 0.6 * math.exp(-0.3 * l)
        w1, b1, wz, bz = _proj_weights(w_in[l], b_in[l], d)
        lam_args = (lambda_q1[l][None, :], lambda_k1[l][None, :],
                    lambda_q2[l][None, :], lambda_k2[l][None, :])
        merge_w = (wz, bz, w_pa[l].astype(BF16), w_pb[l].astype(BF16), w_out[l].astype(BF16),
                   ln_g[l][None, :], ln_b[l][None, :])
        o_a, o_b, kv = _prompt_branches(xp, bp, t, w1, b1, lam_args, subln_g[l], tq, tk, tm,
                                        topk_p, lambda_init)
        pk.append(kv)
        xp = _merge(xp, o_a, o_b, *merge_w, tm, alpha)
        o_a, o_b, kv = _sample_branches(xs, bs, ts, page_table, cache_k_a[l], cache_v_a[l],
                                        cache_k_idx[l], cache_k_b[l], cache_v_b[l], w1, b1,
                                        lam_args, subln_g[l], topk_s, lambda_init)
        sk.append(kv)
        xs = _merge(xs, o_a, o_b, *merge_w, bs * ts, alpha)

    def stacked(layers, b, n):
        shapes = ((b, n, H_A, HD_A), (b, n, H_A, HD_A), (b, n, D_I),
                  (b, n, H_B, 2 * HD_B), (b, n, H_B, 2 * HD_B))
        return tuple(jnp.stack([kv[i].reshape(s) for kv in layers]) for i, s in enumerate(shapes))

    return (xp.reshape(bp, t, d), xs.reshape(bs, ts, d)) + stacked(pk, bp, t) + stacked(sk, bs, ts)
```

```python
import functools
import math

import jax
import jax.numpy as jnp
from jax import lax
from jax.experimental import pallas as pl
from jax.experimental.pallas import tpu as pltpu

F32 = jnp.float32
BF16 = jnp.bfloat16
I32 = jnp.int32

H_A, HD_A = 8, 64
H_I, D_I = 8, 32
H_B, HD_B = 4, 64
TOPK_MAX = 256
PAGE = 128
LN_EPS = 1e-5
RMS_EPS = 1e-5
NEG = -1e30
PAD_SCORE = -3e38
LOG2E = math.log2(math.e)

W_A = H_A * HD_A
W_B = H_B * 2 * HD_B
LANES = 128
SUBLANES = 8
VMEM_LIMIT = 56 * 1024 * 1024
TIE_REFINE_STEPS = 16
PAGES_PER_STEP = 16
IDX_PAGES_PER_STEP = 16
CHAINS_PER_STAGE = 2
PART_ROWS = 64

SPLITS = (
    ('q_a', W_A), ('k_a', W_A), ('v_a', W_A),
    ('q_idx', H_I * D_I), ('k_idx', D_I), ('w_idx', H_I), ('z_a', W_A),
    ('q_b', W_B), ('k_b', W_B), ('v_b', W_B), ('z_b', W_B),
    ('g_a', None), ('g_b', None),
)


def _slopes2(n):
    return tuple(LOG2E * 2.0 ** (-8.0 * (i + 1) / n) for i in range(n))


def _cparams(sem):
    return pltpu.CompilerParams(dimension_semantics=sem, vmem_limit_bytes=VMEM_LIMIT)


def _dot_nt(a, b):
    return lax.dot_general(a, b, (((1,), (1,)), ((), ())), preferred_element_type=F32)


def _dot(a, b):
    return jnp.dot(a, b, preferred_element_type=F32)


_C_QA, _C_KA, _C_VA, _C_QB, _C_KB, _C_VB = 0, 512, 1024, 1536, 2048, 2560
_C_QI, _C_KI, _C_WI, _C_END = 3072, 3328, 3456, 3584


def _proj_kernel(x_ref, w_ref, b_ref, *out_refs, key_major):
    x = x_ref[...].astype(BF16)

    def proj(c0, n):
        return _dot_nt(x, w_ref[c0:c0 + n, :]) + b_ref[:, c0:c0 + n]

    qa = (proj(_C_QA, W_A) * (LOG2E * HD_A ** -0.5)).astype(BF16)
    ka, va = proj(_C_KA, W_A), proj(_C_VA, W_A)
    qb = (proj(_C_QB, W_B) * (LOG2E * HD_B ** -0.5)).astype(BF16)
    kb, vb = proj(_C_KB, W_B), proj(_C_VB, W_B)
    qi = proj(_C_QI, H_I * D_I)
    ki = proj(_C_KI, LANES)[:, :D_I]
    wi = proj(_C_WI, LANES) * ((H_I ** -0.5) * (D_I ** -0.5))
    if key_major:
        (qa_ref, ka32_ref, ka16_ref, va32_ref, vat_ref, qb_ref, kb32_ref, kb16_ref, vb32_ref,
         vbt_ref, qit_ref, ki32_ref, ki16_ref, wit_ref) = out_refs
        ka16_ref[...] = ka.astype(BF16)
        kb16_ref[...] = kb.astype(BF16)
        vat_ref[0, 0] = va.T.astype(BF16)
        vbt_ref[0, 0] = vb.T.astype(BF16)
        qit_ref[0] = qi.T.astype(BF16)
        ki16_ref[...] = ki.astype(BF16)
        wit_ref[0] = wi.T[:H_I, :]
    else:
        qa_ref, ka32_ref, va32_ref, qb_ref, kb32_ref, vb32_ref, qi_ref, ki32_ref, wi_ref = out_refs
        qi_ref[...] = qi.astype(BF16)
        wi_ref[...] = wi[:, :H_I]
    qa_ref[...] = qa
    ka32_ref[...] = ka
    va32_ref[...] = va
    qb_ref[...] = qb
    kb32_ref[...] = kb
    vb32_ref[...] = vb
    ki32_ref[...] = ki


def _project(x2d, w1t, b1, tm, batch=None):
    n, d = x2d.shape
    sds = jax.ShapeDtypeStruct
    row = lambda w, dt: (pl.BlockSpec((tm, w), lambda i: (i, 0)), sds((n, w), dt))
    full = lambda a: pl.BlockSpec(a.shape, lambda i: (0, 0))
    if batch is None:
        outs = [row(W_A, BF16), row(W_A, F32), row(W_A, F32), row(W_B, BF16), row(W_B, F32),
                row(W_B, F32), row(H_I * D_I, BF16), row(D_I, F32), row(H_I, F32)]
    else:
        t = n // batch
        nk = t // tm
        blocks = lambda w: (pl.BlockSpec((1, 1, w, tm), lambda i: (i // nk, i % nk, 0, 0)),
                            sds((batch, nk, w, tm), BF16))
        feat = lambda f, dt: (pl.BlockSpec((1, f, tm), lambda i: (i // nk, 0, i % nk)),
                              sds((batch, f, t), dt))
        outs = [row(W_A, BF16), row(W_A, F32), row(W_A, BF16), row(W_A, F32), blocks(W_A),
                row(W_B, BF16), row(W_B, F32), row(W_B, BF16), row(W_B, F32), blocks(W_B),
                feat(H_I * D_I, BF16), row(D_I, F32), row(D_I, BF16), feat(H_I, F32)]
    return pl.pallas_call(
        functools.partial(_proj_kernel, key_major=batch is not None),
        grid=(n // tm,),
        in_specs=[pl.BlockSpec((tm, d), lambda i: (i, 0)), full(w1t), full(b1)],
        out_specs=[o[0] for o in outs],
        out_shape=[o[1] for o in outs],
        compiler_params=_cparams(("parallel",)),
        name="proj",
    )(x2d, w1t, b1)


def _float_of_key(u):
    s = u ^ jnp.int32(-2 ** 31)
    fb = jnp.where(s >= 0, s, s ^ jnp.int32(0x7FFFFFFF))
    return lax.bitcast_convert_type(fb, F32)


def _kth_largest_cell(count_ge, shape, k):
    def bit_body(i, prefix):
        bit = lax.shift_left(jnp.int32(1), jnp.int32(31) - i)
        cand = prefix | bit
        return jnp.where(count_ge(_float_of_key(cand)) >= k, cand, prefix)

    prefix = lax.fori_loop(0, 32, bit_body, jnp.zeros(shape, I32))

    def refine(_, c):
        lo, hi = c
        mid = lo + (hi - lo) * 0.5
        ok = count_ge(mid) >= k
        return jnp.where(ok, mid, lo), jnp.where(ok, hi, mid)

    return lax.fori_loop(0, TIE_REFINE_STEPS, refine,
                         (_float_of_key(prefix), _float_of_key(prefix + 1)))


def _fold_rows(x, op):
    out = x[:PART_ROWS]
    for i in range(1, x.shape[0] // PART_ROWS):
        out = op(out, x[i * PART_ROWS:(i + 1) * PART_ROWS])
    return out


def _chain_pipeline(n_chain, nkb, tq, acc_rows, scores, offset, values, s_scr, p_scr):
    g = CHAINS_PER_STAGE
    n_group = n_chain // g
    neg0 = jnp.full((PART_ROWS, tq), NEG, F32)
    zero = jnp.zeros((PART_ROWS, tq), F32)
    zacc = jnp.zeros((acc_rows, tq), F32)
    ms, ls, accs = [None] * n_chain, [None] * n_chain, [None] * n_chain
    for grp in range(n_group + 2):
        def body(kb, carry, grp=grp):
            m_a, l_b, acc_c = list(carry[0]), list(carry[1]), list(carry[2])
            for i in range(g):
                if grp >= 2:
                    c = (grp - 2) * g + i
                    acc_c[i] = acc_c[i] + _dot(values(c, kb), p_scr[i, kb])
                if 1 <= grp <= n_group:
                    c = (grp - 1) * g + i
                    p = jnp.exp2(s_scr[i, kb] - (ms[c] - offset(c, kb)))
                    p_scr[i, kb] = p.astype(BF16)
                    l_b[i] = l_b[i] + _fold_rows(p, jnp.add)
                if grp < n_group:
                    c = grp * g + i
                    s = scores(c, kb)
                    s_scr[i, kb] = s
                    m_a[i] = jnp.maximum(m_a[i], _fold_rows(s, jnp.maximum) + offset(c, kb))
            return tuple(m_a), tuple(l_b), tuple(acc_c)

        m_a, l_b, acc_c = lax.fori_loop(0, nkb, body, ((neg0,) * g, (zero,) * g, (zacc,) * g))
        for i in range(g):
            if grp < n_group:
                ms[grp * g + i] = jnp.max(m_a[i], axis=0, keepdims=True)
            if 1 <= grp <= n_group:
                ls[(grp - 1) * g + i] = jnp.sum(l_b[i], axis=0, keepdims=True)
            if grp >= 2:
                accs[(grp - 2) * g + i] = acc_c[i]
    return ls, accs


def _block_geometry(qi, tq, tk):
    nkb = ((qi + 1) * tq + tk - 1) // tk
    kl = lax.broadcasted_iota(I32, (tk, tq), 0)
    ql = lax.broadcasted_iota(I32, (tk, tq), 1)

    def koff(kb):
        return kb * tk - qi * tq + jnp.zeros((1, tq), I32)

    return nkb, kl, ql, koff


def _prompt_a_kernel(q_ref, k_ref, vt_ref, qit_ref, wt_ref, ki_ref, o_ref,
                     sc_scr, s_scr, p_scr, ab_scr, ot_scr, *, tq, tk, topk):
    qi = pl.program_id(1)
    nkb, kl, ql, koff = _block_geometry(qi, tq, tk)
    last = nkb - 1
    causal_last = (kl + koff(last)) <= ql
    wt = wt_ref[0]
    k = float(topk)

    def idx_body(kb, c):
        ki = ki_ref[0, pl.ds(pl.multiple_of(kb * tk, tk), tk), :]
        acc = jnp.zeros((tk, tq), F32)
        for h in range(H_I):
            r = _dot(ki, qit_ref[0, h * D_I:(h + 1) * D_I, :])
            acc = acc + jnp.maximum(r, 0.0) * wt[h:h + 1, :]
        sc_scr[kb] = acc
        return c

    lax.fori_loop(0, nkb, idx_body, 0)
    sc_scr[last] = jnp.where(causal_last, sc_scr[last], NEG)

    def count_ge(v):
        def body(kb, cnt):
            t = sc_scr[kb]
            for i in range(tk // PART_ROWS):
                cnt = jnp.where(t[i * PART_ROWS:(i + 1) * PART_ROWS] >= v, cnt + 1.0, cnt)
            return cnt
        part = lax.fori_loop(0, nkb, body, jnp.zeros((PART_ROWS, tq), F32))
        return jnp.sum(part, axis=0, keepdims=True)

    def bit_body(i, prefix):
        bit = lax.shift_left(jnp.int32(1), jnp.int32(31) - i)
        cand = prefix | bit
        return jnp.where(count_ge(_float_of_key(cand)) >= k, cand, prefix)

    prefix = lax.fori_loop(0, 32, bit_body, jnp.zeros((1, tq), I32))
    thr = _float_of_key(prefix)
    tie = jnp.max(count_ge(thr)) > k

    @pl.when(jnp.logical_not(tie))
    def _():
        def sel_body(kb, c):
            sc_scr[kb] = jnp.where(sc_scr[kb] >= thr, 0.0, NEG)
            return c
        lax.fori_loop(0, nkb, sel_body, 0)

    @pl.when(tie)
    def _():
        def refine(_, c):
            lo, hi = c
            mid = lo + (hi - lo) * 0.5
            ok = count_ge(mid) >= k
            return jnp.where(ok, mid, lo), jnp.where(ok, hi, mid)

        lo, hi = lax.fori_loop(0, TIE_REFINE_STEPS, refine, (thr, _float_of_key(prefix + 1)))
        need = k - count_ge(hi)
        below = (lax.broadcasted_iota(I32, (tk, tk), 1)
                 < lax.broadcasted_iota(I32, (tk, tk), 0)).astype(BF16)

        def sel_body(kb, seen):
            t = sc_scr[kb]
            eq = (t >= lo) & (t < hi)
            eqf = jnp.where(eq, 1.0, 0.0)
            rank = _dot(below, eqf.astype(BF16)) + seen
            sc_scr[kb] = jnp.where((t >= hi) | (eq & (rank < need)), 0.0, NEG)
            return seen + jnp.sum(eqf, axis=0, keepdims=True)

        lax.fori_loop(0, nkb, sel_body, jnp.zeros((1, tq), F32))

    sc_scr[last] = jnp.where(causal_last, sc_scr[last], NEG)

    slopes = _slopes2(H_A)
    klf = kl.astype(F32)
    lane = lax.broadcasted_iota(I32, (tq, LANES), 1)
    qms = []
    for h in range(H_A):
        pair, odd = h // 2, h % 2
        qp = q_ref[0, :, pair * LANES:(pair + 1) * LANES]
        qms.append(jnp.where((lane >= HD_A) == bool(odd), qp, jnp.zeros_like(qp)))
        ab_scr[h] = slopes[h] * klf

    def scores(h, kb):
        cols = slice((h // 2) * LANES, (h // 2 + 1) * LANES)
        kblk = k_ref[0, pl.ds(pl.multiple_of(kb * tk, tk), tk), cols]
        return _dot_nt(kblk, qms[h]) + (sc_scr[kb] + ab_scr[h])

    def offset(h, kb):
        return slopes[h] * koff(kb).astype(F32)

    def values(h, kb):
        return vt_ref[0, kb, h * HD_A:(h + 1) * HD_A, :]

    ls, accs = _chain_pipeline(H_A, nkb, tq, HD_A, scores, offset, values, s_scr, p_scr)
    for h in range(H_A):
        ot_scr[h * HD_A:(h + 1) * HD_A, :] = accs[h] / ls[h]
    o_ref[0] = ot_scr[...].T.astype(o_ref.dtype)


def _prompt_a(qa, ka, vta, qit, wt, ki, tq, tk, topk):
    b, t, _ = qa.shape
    nq, nk = t // tq, t // tk
    kern = functools.partial(_prompt_a_kernel, tq=tq, tk=tk, topk=topk)
    return pl.pallas_call(
        kern,
        grid=(b, nq),
        in_specs=[
            pl.BlockSpec((1, tq, W_A), lambda i, j: (i, j, 0)),
            pl.BlockSpec((1, t, W_A), lambda i, j: (i, 0, 0)),
            pl.BlockSpec((1, nk, W_A, tk), lambda i, j: (i, 0, 0, 0)),
            pl.BlockSpec((1, H_I * D_I, tq), lambda i, j: (i, 0, j)),
            pl.BlockSpec((1, H_I, tq), lambda i, j: (i, 0, j)),
            pl.BlockSpec((1, t, D_I), lambda i, j: (i, 0, 0)),
        ],
        out_specs=pl.BlockSpec((1, tq, W_A), lambda i, j: (i, j, 0)),
        out_shape=jax.ShapeDtypeStruct((b, t, W_A), F32),
        scratch_shapes=[
            pltpu.VMEM((nk, tk, tq), F32),
            pltpu.VMEM((CHAINS_PER_STAGE, nk, tk, tq), F32),
            pltpu.VMEM((CHAINS_PER_STAGE, nk, tk, tq), BF16),
            pltpu.VMEM((H_A, tk, tq), F32),
            pltpu.VMEM((W_A, tq), F32),
        ],
        compiler_params=_cparams(("parallel", "arbitrary")),
        name="prompt_a",
    )(qa, ka, vta, qit, wt, ki)


def _lambda_of(lq1_ref, lk1_ref, lq2_ref, lk2_ref, lambda_init):
    s1 = jnp.sum(lq1_ref[...] * lk1_ref[...], axis=1, keepdims=True)
    s2 = jnp.sum(lq2_ref[...] * lk2_ref[...], axis=1, keepdims=True)
    return jnp.exp(s1) - jnp.exp(s2) + lambda_init


def _prompt_b_kernel(q_ref, k_ref, vt_ref, lq1_ref, lk1_ref, lq2_ref, lk2_ref, g_ref, o_ref,
                     s_scr, p_scr, ab_scr, ot_scr, *, tq, tk, lambda_init):
    qi = pl.program_id(1)
    nkb, kl, ql, koff = _block_geometry(qi, tq, tk)
    last = nkb - 1
    causal_last = (kl + koff(last)) <= ql
    klf = kl.astype(F32)
    lane = lax.broadcasted_iota(I32, (tq, LANES), 1)
    lam = _lambda_of(lq1_ref, lk1_ref, lq2_ref, lk2_ref, lambda_init)
    slopes = _slopes2(H_B)
    qms = []
    for h in range(H_B):
        qp = q_ref[0, :, h * LANES:(h + 1) * LANES]
        qms.append(jnp.where(lane < HD_B, qp, jnp.zeros_like(qp)))
        qms.append(jnp.where(lane >= HD_B, qp, jnp.zeros_like(qp)))
        a_h = slopes[h] * klf
        ab_scr[0, h] = a_h
        ab_scr[1, h] = jnp.where(causal_last, a_h, NEG)

    def scores(c, kb):
        h = c // 2
        kblk = k_ref[0, pl.ds(pl.multiple_of(kb * tk, tk), tk), h * LANES:(h + 1) * LANES]
        return _dot_nt(kblk, qms[c]) + ab_scr[(kb == last).astype(I32), h]

    def offset(c, kb):
        return slopes[c // 2] * koff(kb).astype(F32)

    def values(c, kb):
        h = c // 2
        return vt_ref[0, kb, h * LANES:(h + 1) * LANES, :]

    ls, accs = _chain_pipeline(2 * H_B, nkb, tq, LANES, scores, offset, values, s_scr, p_scr)
    g = g_ref[...]
    for h in range(H_B):
        o = accs[2 * h] / ls[2 * h] - lam * (accs[2 * h + 1] / ls[2 * h + 1])
        ms = jnp.mean(o * o, axis=0, keepdims=True)
        o = o * lax.rsqrt(ms + RMS_EPS) * g
        ot_scr[h * LANES:(h + 1) * LANES, :] = o * (1.0 - lambda_init)
    o_ref[0] = ot_scr[...].T.astype(o_ref.dtype)


def _prompt_b(qb, kb, vtb, lq1, lk1, lq2, lk2, g_col, tq, tk, lambda_init):
    b, t, _ = qb.shape
    nq, nk = t // tq, t // tk
    kern = functools.partial(_prompt_b_kernel, tq=tq, tk=tk, lambda_init=lambda_init)
    small = lambda a: pl.BlockSpec(a.shape, lambda i, j: (0, 0))
    return pl.pallas_call(
        kern,
        grid=(b, nq),
        in_specs=[
            pl.BlockSpec((1, tq, W_B), lambda i, j: (i, j, 0)),
            pl.BlockSpec((1, t, W_B), lambda i, j: (i, 0, 0)),
            pl.BlockSpec((1, nk, W_B, tk), lambda i, j: (i, 0, 0, 0)),
            small(lq1), small(lk1), small(lq2), small(lk2), small(g_col),
        ],
        out_specs=pl.BlockSpec((1, tq, W_B), lambda i, j: (i, j, 0)),
        out_shape=jax.ShapeDtypeStruct((b, t, W_B), F32),
        scratch_shapes=[
            pltpu.VMEM((CHAINS_PER_STAGE, nk, tk, tq), F32),
            pltpu.VMEM((CHAINS_PER_STAGE, nk, tk, tq), BF16),
            pltpu.VMEM((2, H_B, tk, tq), F32),
            pltpu.VMEM((W_B, tq), F32),
        ],
        compiler_params=_cparams(("parallel", "arbitrary")),
        name="prompt_b",
    )(qb, kb, vtb, lq1, lk1, lq2, lk2, g_col)


def _sigmoid(x):
    return 1.0 / (1.0 + jnp.exp(-x))


def _merge_kernel(x_ref, oa_ref, ob_ref, wz_ref, bz_ref, wpa_ref, wpb_ref, wout_ref,
                  lng_ref, lnb_ref, y_ref, *, alpha):
    x = x_ref[...]
    xh = x.astype(BF16)

    def proj(c0, n):
        return _dot_nt(xh, wz_ref[c0:c0 + n, :]) + bz_ref[:, c0:c0 + n]

    d = x.shape[1]
    za = proj(0, W_A)
    ua = (oa_ref[...] * (za * _sigmoid(za))).astype(BF16)
    ya = _dot(ua, wpa_ref[...])
    zb = proj(W_A, W_B)
    ub = (ob_ref[...] * (zb * _sigmoid(zb))).astype(BF16)
    yb = _dot(ub, wpb_ref[...])
    m = _sigmoid(proj(W_A + W_B, d)) * ya + _sigmoid(proj(W_A + W_B + d, d)) * yb
    r = alpha * x + _dot(m.astype(BF16), wout_ref[...])
    mu = jnp.mean(r, axis=1, keepdims=True)
    rc = r - mu
    var = jnp.mean(rc * rc, axis=1, keepdims=True)
    y_ref[...] = rc * lax.rsqrt(var + LN_EPS) * lng_ref[...] + lnb_ref[...]


def _merge(x2d, oa, ob, wz, bz, wpa, wpb, wout, lng, lnb, tm, alpha):
    n, d = x2d.shape
    row = lambda w: pl.BlockSpec((tm, w), lambda i: (i, 0))
    full = lambda a: pl.BlockSpec(a.shape, lambda i: (0, 0))
    return pl.pallas_call(
        functools.partial(_merge_kernel, alpha=alpha),
        grid=(n // tm,),
        in_specs=[row(d), row(W_A), row(W_B), full(wz), full(bz), full(wpa), full(wpb),
                  full(wout), full(lng), full(lnb)],
        out_specs=row(d),
        out_shape=jax.ShapeDtypeStruct((n, d), F32),
        compiler_params=_cparams(("parallel",)),
        name="merge",
    )(x2d, oa, ob, wz, bz, wpa, wpb, wout, lng, lnb)


def _page_specs(n, shape, first_page):
    def spec(j):
        return pl.BlockSpec((1,) + shape, lambda i, s, pt, j=j: (pt[i, first_page(s) + j], 0, 0))
    return [spec(j) for j in range(n)]


def _sample_idx_kernel(pt_ref, qi_ref, wcol_ref, kin_ref, *rest, n_pages, pps, n_tok, topk):
    kip_refs, (bias_ref, sc_scr) = rest[:pps], rest[pps:]
    step = pl.program_id(1)
    n_steps = n_pages // pps
    lane = lax.broadcasted_iota(I32, (SUBLANES, PAGE), 1)
    trow = lax.broadcasted_iota(I32, (SUBLANES, PAGE), 0)
    k = float(topk)

    def scores(kpages_t):
        r = _dot(qi_ref[0], kpages_t.astype(BF16))
        r = jnp.maximum(r, 0.0) * wcol_ref[0]
        per_tok = [jnp.sum(r[t * H_I:(t + 1) * H_I, :], axis=0, keepdims=True)
                   for t in range(n_tok)]
        pad = [jnp.zeros_like(per_tok[0])] * (SUBLANES - n_tok)
        return jnp.concatenate(per_tok + pad, axis=0)

    s_all = scores(jnp.concatenate([r[0] for r in kip_refs], axis=1))
    for j in range(pps):
        sc_scr[step * pps + j] = s_all[:, j * PAGE:(j + 1) * PAGE]

    @pl.when(step == n_steps - 1)
    def _():
        new_ok = (lane <= trow) & (lane < n_tok)
        s = jnp.where(lane <= trow, scores(kin_ref[0]), NEG)
        sc_scr[n_pages] = jnp.where(lane < n_tok, s, PAD_SCORE)

        def count_ge(v):
            parts = [jnp.zeros((SUBLANES, PAGE), F32)] * 4
            for j in range(n_pages + 1):
                parts[j % 4] = parts[j % 4] + jnp.where(sc_scr[j] >= v, 1.0, 0.0)
            return jnp.sum((parts[0] + parts[1]) + (parts[2] + parts[3]), axis=1, keepdims=True)

        lo, hi = _kth_largest_cell(count_ge, (SUBLANES, 1), k)
        need = k - count_ge(hi)
        before = (lax.broadcasted_iota(I32, (PAGE, PAGE), 0)
                  < lax.broadcasted_iota(I32, (PAGE, PAGE), 1)).astype(BF16)
        seen = jnp.zeros((SUBLANES, 1), F32)
        for j in range(n_pages + 1):
            t = sc_scr[j]
            eq = (t >= lo) & (t < hi)
            eqf = jnp.where(eq, 1.0, 0.0)
            rank = _dot(eqf.astype(BF16), before) + seen
            sel = (t >= hi) | (eq & (rank < need))
            if j == n_pages:
                sel = sel & new_ok
            bias_ref[0, j] = jnp.where(sel, 0.0, NEG)
            seen = seen + jnp.sum(eqf, axis=1, keepdims=True)


def _sample_idx(page_table, qi_rows, wcol, cache_ki_t, ki_new_t, n_tok, topk):
    b, n_pages = page_table.shape
    rows = n_tok * H_I
    pps = min(IDX_PAGES_PER_STEP, n_pages)
    kern = functools.partial(_sample_idx_kernel, n_pages=n_pages, pps=pps, n_tok=n_tok,
                             topk=topk)
    grid_spec = pltpu.PrefetchScalarGridSpec(
        num_scalar_prefetch=1,
        grid=(b, n_pages // pps),
        in_specs=[
            pl.BlockSpec((1, rows, D_I), lambda i, s, pt: (i, 0, 0)),
            pl.BlockSpec((1, rows, 1), lambda i, s, pt: (i, 0, 0)),
            pl.BlockSpec((1, D_I, PAGE), lambda i, s, pt: (i, 0, 0)),
        ] + _page_specs(pps, (D_I, PAGE), lambda s: s * pps),
        out_specs=pl.BlockSpec((1, n_pages + 1, SUBLANES, PAGE), lambda i, s, pt: (i, 0, 0, 0)),
        scratch_shapes=[pltpu.VMEM((n_pages + 1, SUBLANES, PAGE), F32)],
    )
    return pl.pallas_call(
        kern,
        grid_spec=grid_spec,
        out_shape=jax.ShapeDtypeStruct((b, n_pages + 1, SUBLANES, PAGE), F32),
        compiler_params=_cparams(("parallel", "arbitrary")),
        name="sample_idx",
    )(page_table, qi_rows, wcol, ki_new_t, *([cache_ki_t] * pps))


def _softmax_update(s, m_scr, l_scr):
    m = m_scr[...]
    m_new = jnp.maximum(m, jnp.max(s, axis=1, keepdims=True))
    p = jnp.exp2(s - m_new)
    alpha = jnp.exp2(m - m_new)
    l_scr[...] = alpha * l_scr[...] + jnp.sum(p, axis=1, keepdims=True)
    m_scr[...] = m_new
    return p, alpha


def _decode_a_kernel(pt_ref, wq_ref, slope_ref, kn_ref, vn_ref, bias_ref, biasn_ref, *rest,
                     n_pages, pps, n_tok, past_len):
    kp_refs, vp_refs = rest[:pps], rest[pps:2 * pps]
    o_ref, acc_scr, m_scr, l_scr = rest[2 * pps:]
    step = pl.program_id(1)
    n_steps = n_pages // pps
    rows = n_tok * H_A

    @pl.when(step == 0)
    def _():
        acc_scr[...] = jnp.zeros_like(acc_scr)
        m_scr[...] = jnp.full_like(m_scr, NEG)
        l_scr[...] = jnp.zeros_like(l_scr)

    def update(k_ts, v_ts, first_pos, b8s, new_keys):
        n = len(k_ts)
        lane = lax.broadcasted_iota(I32, (rows, n * PAGE), 1)
        trow = lax.broadcasted_iota(I32, (rows, n * PAGE), 0) // H_A
        s = jnp.concatenate([_dot(wq_ref[0], k_t.astype(BF16)) for k_t in k_ts], axis=1)
        s = s - slope_ref[...] * (past_len + trow - (first_pos + lane)).astype(F32)
        s = s + jnp.concatenate(
            [jnp.broadcast_to(jnp.concatenate([b8[t:t + 1, :] for b8 in b8s], axis=1),
                              (H_A, n * PAGE)) for t in range(n_tok)], axis=0)
        if new_keys:
            s = jnp.where((lane <= trow) & (lane < n_tok), s, NEG)
        p, alpha = _softmax_update(s, m_scr, l_scr)
        p = p.astype(BF16)
        pv = _dot_nt(p[:, :PAGE], v_ts[0].astype(BF16))
        for j in range(1, n):
            pv = pv + _dot_nt(p[:, j * PAGE:(j + 1) * PAGE], v_ts[j].astype(BF16))
        acc_scr[...] = acc_scr[...] * alpha + pv

    update([r[0] for r in kp_refs], [r[0] for r in vp_refs], step * (pps * PAGE),
           [bias_ref[0, j] for j in range(pps)], False)

    @pl.when(step == n_steps - 1)
    def _():
        update([kn_ref[0]], [vn_ref[0]], past_len, [biasn_ref[0, 0]], True)
        o = acc_scr[...] / l_scr[...]
        own = (lax.broadcasted_iota(I32, (rows, W_A), 1) // HD_A
               == lax.broadcasted_iota(I32, (rows, W_A), 0) % H_A)
        o = jnp.where(own, o, 0.0)
        o_ref[0] = jnp.concatenate(
            [jnp.sum(o[t * H_A:(t + 1) * H_A, :], axis=0, keepdims=True) for t in range(n_tok)],
            axis=0)


def _decode_a(page_table, wq, slope_col, cache_k_t, cache_v_t, k_new_t, v_new_t, bias, n_tok,
              past_len):
    b, n_pages = page_table.shape
    rows = n_tok * H_A
    pps = min(PAGES_PER_STEP, n_pages)
    kern = functools.partial(_decode_a_kernel, n_pages=n_pages, pps=pps, n_tok=n_tok,
                             past_len=past_len)
    pages = _page_specs(pps, (W_A, PAGE), lambda s: s * pps)
    per_batch = lambda shape: pl.BlockSpec((1,) + shape, lambda i, s, pt: (i,) + (0,) * len(shape))
    grid_spec = pltpu.PrefetchScalarGridSpec(
        num_scalar_prefetch=1,
        grid=(b, n_pages // pps),
        in_specs=[
            per_batch((rows, W_A)),
            pl.BlockSpec((rows, 1), lambda i, s, pt: (0, 0)),
            per_batch((W_A, PAGE)), per_batch((W_A, PAGE)),
            pl.BlockSpec((1, pps, SUBLANES, PAGE), lambda i, s, pt: (i, s, 0, 0)),
            pl.BlockSpec((1, 1, SUBLANES, PAGE), lambda i, s, pt: (i, n_pages, 0, 0)),
        ] + pages + pages,
        out_specs=per_batch((n_tok, W_A)),
        scratch_shapes=[pltpu.VMEM((rows, W_A), F32), pltpu.VMEM((rows, 1), F32),
                        pltpu.VMEM((rows, 1), F32)],
    )
    return pl.pallas_call(
        kern,
        grid_spec=grid_spec,
        out_shape=jax.ShapeDtypeStruct((b, n_tok, W_A), F32),
        compiler_params=_cparams(("parallel", "arbitrary")),
        name="decode_a",
    )(page_table, wq, slope_col, k_new_t, v_new_t, bias, bias,
      *([cache_k_t] * pps), *([cache_v_t] * pps))


def _decode_b_kernel(pt_ref, wq_ref, kn_ref, vn_ref, lq1_ref, lk1_ref, lq2_ref, lk2_ref, g_ref,
                     *rest, n_pages, pps, n_tok, past_len, lambda_init):
    kp_refs, vp_refs = rest[:pps], rest[pps:2 * pps]
    o_ref, acc_scr, m_scr, l_scr = rest[2 * pps:]
    step = pl.program_id(1)
    n_steps = n_pages // pps
    hrows = 2 * n_tok
    rows = H_B * hrows
    slope_col = jnp.concatenate([jnp.full((hrows, 1), s, F32) for s in _slopes2(H_B)], axis=0)

    @pl.when(step == 0)
    def _():
        acc_scr[...] = jnp.zeros_like(acc_scr)
        m_scr[...] = jnp.full_like(m_scr, NEG)
        l_scr[...] = jnp.zeros_like(l_scr)

    def update(k_hs, v_hs, first_pos, new_keys):
        n = len(k_hs[0])
        lane = lax.broadcasted_iota(I32, (rows, n * PAGE), 1)
        trow = lax.broadcasted_iota(I32, (rows, n * PAGE), 0) % n_tok
        s = jnp.concatenate(
            [jnp.concatenate([_dot_nt(wq_ref[0, h], k.astype(BF16)) for k in k_hs[h]], axis=1)
             for h in range(H_B)], axis=0)
        s = s - slope_col * (past_len + trow - (first_pos + lane)).astype(F32)
        if new_keys:
            s = jnp.where((lane <= trow) & (lane < n_tok), s, NEG)
        p, alpha = _softmax_update(s, m_scr, l_scr)
        p = p.astype(BF16)
        pvs = []
        for h in range(H_B):
            ph = p[h * hrows:(h + 1) * hrows, :]
            pv = _dot(ph[:, :PAGE], v_hs[h][0].astype(BF16))
            for j in range(1, n):
                pv = pv + _dot(ph[:, j * PAGE:(j + 1) * PAGE], v_hs[h][j].astype(BF16))
            pvs.append(pv)
        acc_scr[...] = acc_scr[...] * alpha + jnp.concatenate(pvs, axis=0)

    head_rows = lambda refs, h: [r[0, pl.ds(h, PAGE, stride=H_B), :] for r in refs]
    update([head_rows(kp_refs, h) for h in range(H_B)],
           [head_rows(vp_refs, h) for h in range(H_B)], step * (pps * PAGE), False)

    @pl.when(step == n_steps - 1)
    def _():
        update([[kn_ref[0, h]] for h in range(H_B)], [[vn_ref[0, h]] for h in range(H_B)],
               past_len, True)
        lam = _lambda_of(lq1_ref, lk1_ref, lq2_ref, lk2_ref, lambda_init)
        o_all = acc_scr[...] / l_scr[...]
        for h in range(H_B):
            o = (o_all[h * hrows:h * hrows + n_tok, :]
                 - lam * o_all[h * hrows + n_tok:(h + 1) * hrows, :])
            ms = jnp.mean(o * o, axis=1, keepdims=True)
            o = o * lax.rsqrt(ms + RMS_EPS) * g_ref[...] * (1.0 - lambda_init)
            o_ref[0, :, h * LANES:(h + 1) * LANES] = o


def _decode_b(page_table, wq, cache_k_r, cache_v_r, k_new, v_new, lam_args, g_row, n_tok,
              past_len, lambda_init):
    b, n_pages = page_table.shape
    rows = 2 * n_tok
    pps = min(PAGES_PER_STEP, n_pages)
    kern = functools.partial(_decode_b_kernel, n_pages=n_pages, pps=pps, n_tok=n_tok,
                             past_len=past_len, lambda_init=lambda_init)
    pages = _page_specs(pps, (PAGE * H_B, 2 * HD_B), lambda s: s * pps)
    per_batch = lambda shape: pl.BlockSpec((1,) + shape, lambda i, s, pt: (i,) + (0,) * len(shape))
    small = lambda a: pl.BlockSpec(a.shape, lambda i, s, pt: (0, 0))
    grid_spec = pltpu.PrefetchScalarGridSpec(
        num_scalar_prefetch=1,
        grid=(b, n_pages // pps),
        in_specs=[
            per_batch((H_B, rows, 2 * HD_B)),
            per_batch((H_B, PAGE, 2 * HD_B)), per_batch((H_B, PAGE, 2 * HD_B)),
        ] + [small(a) for a in lam_args] + [small(g_row)] + pages + pages,
        out_specs=per_batch((n_tok, W_B)),
        scratch_shapes=[pltpu.VMEM((H_B * rows, 2 * HD_B), F32), pltpu.VMEM((H_B * rows, 1), F32),
                        pltpu.VMEM((H_B * rows, 1), F32)],
    )
    return pl.pallas_call(
        kern,
        grid_spec=grid_spec,
        out_shape=jax.ShapeDtypeStruct((b, n_tok, W_B), F32),
        compiler_params=_cparams(("parallel", "arbitrary")),
        name="decode_b",
    )(page_table, wq, k_new, v_new, *lam_args, g_row,
      *([cache_k_r] * pps), *([cache_v_r] * pps))


def _split_rows(wt, d_model):
    out, off = {}, 0
    for name, n in SPLITS:
        n = d_model if n is None else n
        out[name] = wt[off:off + n]
        off += n
    return out


def _pad_rows(a, rows):
    return jnp.pad(a, [(0, rows - a.shape[0])] + [(0, 0)] * (a.ndim - 1))


def _proj_weights(w_in, b_in, d_model):
    w = _split_rows(w_in.T, d_model)
    b = _split_rows(b_in, d_model)
    order1 = ['q_a', 'k_a', 'v_a', 'q_b', 'k_b', 'v_b', 'q_idx']
    group1 = lambda p: jnp.concatenate(
        [p[n] for n in order1] + [_pad_rows(p['k_idx'], LANES), _pad_rows(p['w_idx'], LANES)])
    order2 = ['z_a', 'z_b', 'g_a', 'g_b']
    group2 = lambda p: jnp.concatenate([p[n] for n in order2])
    return (group1(w).astype(BF16), group1(b)[None, :], group2(w).astype(BF16),
            group2(b)[None, :])


def _prompt_branches(xp, bp, t, w1t, b1, lam_args, g, tq, tk, topk, lambda_init):
    (qa, ka32, ka16, va32, vat, qb, kb32, kb16, vb32, vbt, qit, ki32, ki16, wit) = _project(
        xp, w1t, b1, tk, batch=bp)
    r3 = lambda a: a.reshape(bp, t, a.shape[-1])
    o_a = _prompt_a(r3(qa), r3(ka16), vat, qit, wit, r3(ki16), tq, tk, topk)
    o_b = _prompt_b(r3(qb), r3(kb16), vbt, *lam_args, g[:, None], tq, tk, lambda_init)
    return o_a.reshape(bp * t, W_A), o_b.reshape(bp * t, W_B), (ka32, va32, ki32, kb32, vb32)


def _sample_branches(xs, bs, ts, page_table, c_k_a, c_v_a, c_k_idx, c_k_b, c_v_b, w1t, b1,
                     lam_args, g, topk, lambda_init):
    n_pool = c_k_a.shape[0]
    past_len = page_table.shape[1] * PAGE
    qa, ka32, va32, qb, kb32, vb32, qidx, ki32, wi = _project(xs, w1t, b1, bs * ts)
    s3 = lambda a: a.reshape(bs, ts, a.shape[-1])
    new_t = lambda a: jnp.pad(s3(a).transpose(0, 2, 1), ((0, 0), (0, 0), (0, PAGE - ts)))
    qi_rows = s3(qidx).reshape(bs, ts * H_I, D_I)
    wcol = s3(wi).reshape(bs, ts * H_I, 1)
    bias = _sample_idx(page_table, qi_rows, wcol, c_k_idx.transpose(0, 2, 1), new_t(ki32), ts,
                       topk)
    eye_a = jnp.eye(H_A, dtype=BF16)
    wq_a = (s3(qa).reshape(bs, ts, H_A, 1, HD_A) * eye_a[None, None, :, :, None]
            ).reshape(bs, ts * H_A, W_A)
    slope_a = jnp.tile(jnp.asarray(_slopes2(H_A), F32), ts)[:, None]
    page_t = lambda c: c.transpose(0, 2, 3, 1).reshape(n_pool, W_A, PAGE)
    o_a = _decode_a(page_table, wq_a, slope_a, page_t(c_k_a), page_t(c_v_a), new_t(ka32),
                    new_t(va32), bias, ts, past_len)
    qb4 = s3(qb).reshape(bs, ts, H_B, 2 * HD_B).transpose(0, 2, 1, 3)
    first = (jnp.arange(2 * HD_B) < HD_B)
    wq_b = jnp.concatenate([jnp.where(first, qb4, 0), jnp.where(first, 0, qb4)], axis=2)
    new_h = lambda a: jnp.pad(s3(a).reshape(bs, ts, H_B, 2 * HD_B).transpose(0, 2, 1, 3),
                              ((0, 0), (0, 0), (0, PAGE - ts), (0, 0)))
    page_r = lambda c: c.reshape(n_pool, PAGE * H_B, 2 * HD_B)
    o_b = _decode_b(page_table, wq_b, page_r(c_k_b), page_r(c_v_b), new_h(kb32), new_h(vb32),
                    lam_args, g[None, :], ts, past_len, lambda_init)
    return (o_a.reshape(bs * ts, W_A), o_b.reshape(bs * ts, W_B),
            (ka32, va32, ki32, kb32, vb32))


def kernel(x_prompt, x_sample, cache_k_a, cache_v_a, cache_k_idx, cache_k_b, cache_v_b, page_table,
           w_in, b_in, lambda_q1, lambda_k1, lambda_q2, lambda_k2, subln_g, w_pa, w_pb, w_out,
           ln_g, ln_b):
    depth = w_in.shape[0]
    bp, t, d = x_prompt.shape
    bs, ts, _ = x_sample.shape
    past_len = page_table.shape[1] * PAGE
    alpha = (2.0 * depth) ** 0.25
    tq = min(256, t)
    tk = min(512, t)
    tm = min(512, bp * t)
    topk_p = min(TOPK_MAX, t // 4)
    topk_s = min(TOPK_MAX, (past_len + ts) // 4)

    xp = x_prompt.reshape(bp * t, d)
    xs = x_sample.reshape(bs * ts, d)
    pk, sk = [], []
    for l in range(depth):
        lambda_init = 0.8 - 0.6 * math.exp(-0.3 * l)
        w1t, b1, wzt, bz = _proj_weights(w_in[l], b_in[l], d)
        lam_args = (lambda_q1[l][None, :], lambda_k1[l][None, :],
                    lambda_q2[l][None, :], lambda_k2[l][None, :])
        merge_w = (wzt, bz, w_pa[l].astype(BF16), w_pb[l].astype(BF16), w_out[l].astype(BF16),
                   ln_g[l][None, :], ln_b[l][None, :])
        o_a, o_b, kv = _prompt_branches(xp, bp, t, w1t, b1, lam_args, subln_g[l], tq, tk, topk_p,
                                        lambda_init)
        pk.append(kv)
        xp = _merge(xp, o_a, o_b, *merge_w, tm, alpha)
        o_a, o_b, kv = _sample_branches(xs, bs, ts, page_table, cache_k_a[l], cache_v_a[l],
                                        cache_k_idx[l], cache_k_b[l], cache_v_b[l], w1t, b1,
                                        lam_args, subln_g[l], topk_s, lambda_init)
        sk.append(kv)
        xs = _merge(xs, o_a, o_b, *merge_w, bs * ts, alpha)

    def stacked(layers, b, n):
        shapes = ((b, n, H_A, HD_A), (b, n, H_A, HD_A), (b, n, D_I),
                  (b, n, H_B, 2 * HD_B), (b, n, H_B, 2 * HD_B))
        return tuple(jnp.stack([kv[i].reshape(s) for kv in layers]) for i, s in enumerate(shapes))

    return (xp.reshape(bp, t, d), xs.reshape(bs, ts, d)) + stacked(pk, bp, t) + stacked(sk, bs, ts)
```

```python
import functools
import math

import jax
import jax.numpy as jnp
import numpy as np
from jax import lax
from jax.experimental import pallas as pl
from jax.experimental.pallas import tpu as pltpu

F32 = jnp.float32
BF16 = jnp.bfloat16
I32 = jnp.int32

H_A, HD_A = 8, 64
H_I, D_I = 8, 32
H_B, HD_B = 4, 64
TOPK_MAX = 256
PAGE = 128
LN_EPS = 1e-5
RMS_EPS = 1e-5
NEG = -1e30
PAD_SCORE = -3e38
LOG2E = math.log2(math.e)

W_A = H_A * HD_A
W_B = H_B * 2 * HD_B
LANES = 128
SUBLANES = 8
VMEM_LIMIT = 56 * 1024 * 1024
TIE_REFINE_STEPS = 16
PAGES_PER_STEP = 16
IDX_PAGES_PER_STEP = 16
CHAINS_PER_STAGE = 2
PART_ROWS = 64

SPLITS = (
    ('q_a', W_A), ('k_a', W_A), ('v_a', W_A),
    ('q_idx', H_I * D_I), ('k_idx', D_I), ('w_idx', H_I), ('z_a', W_A),
    ('q_b', W_B), ('k_b', W_B), ('v_b', W_B), ('z_b', W_B),
    ('g_a', None), ('g_b', None),
)


def _slopes2(n):
    return tuple(LOG2E * 2.0 ** (-8.0 * (i + 1) / n) for i in range(n))


def _cparams(sem):
    return pltpu.CompilerParams(dimension_semantics=sem, vmem_limit_bytes=VMEM_LIMIT)


def _dot_nt(a, b):
    return lax.dot_general(a, b, (((1,), (1,)), ((), ())), preferred_element_type=F32)


def _dot(a, b):
    return jnp.dot(a, b, preferred_element_type=F32)


_C_QA, _C_KA, _C_VA, _C_QB, _C_KB, _C_VB = 0, 512, 1024, 1536, 2048, 2560
_C_QI, _C_KI, _C_WI, _C_END = 3072, 3328, 3456, 3584


def _proj_kernel(x_ref, w_ref, b_ref, *out_refs, key_major):
    x = x_ref[...].astype(BF16)

    def proj(c0, n):
        return _dot_nt(x, w_ref[c0:c0 + n, :]) + b_ref[:, c0:c0 + n]

    qa = proj(_C_QA, W_A) * (LOG2E * HD_A ** -0.5)
    ka, va = proj(_C_KA, W_A), proj(_C_VA, W_A)
    qb = proj(_C_QB, W_B) * (LOG2E * HD_B ** -0.5)
    kb, vb = proj(_C_KB, W_B), proj(_C_VB, W_B)
    qi = proj(_C_QI, H_I * D_I)
    ki = proj(_C_KI, LANES)
    wi = proj(_C_WI, LANES) * ((H_I ** -0.5) * (D_I ** -0.5))
    if key_major:
        (qat_ref, kat32_ref, ka16_ref, vat32_ref, vat_ref, qbt_ref, kb32_ref, kb16_ref, vb32_ref,
         vbt_ref, qit_ref, kit32_ref, ki16_ref, wit_ref) = out_refs
        qat_ref[0] = qa.T.astype(BF16)
        qbt_ref[0] = qb.T.astype(BF16)
        kat32_ref[0] = ka.T
        ka16_ref[...] = ka.astype(BF16)
        kb16_ref[...] = kb.astype(BF16)
        va_t, vb_t = va.T, vb.T
        vat32_ref[0] = va_t
        vat_ref[0, 0] = va_t.astype(BF16)
        vbt_ref[0, 0] = vb_t.astype(BF16)
        qit_ref[0] = qi.T.astype(BF16)
        kit32_ref[0] = ki.T[:D_I, :]
        ki16_ref[...] = ki[:, :D_I].astype(BF16)
        wit_ref[0] = wi.T[:H_I, :]
    else:
        qa_ref, ka32_ref, va32_ref, qb_ref, kb32_ref, vb32_ref, qi_ref, ki32_ref, wi_ref = out_refs
        qa_ref[...] = qa.astype(BF16)
        qb_ref[...] = qb.astype(BF16)
        ka32_ref[...] = ka
        va32_ref[...] = va
        qi_ref[...] = qi.astype(BF16)
        ki32_ref[...] = ki[:, :D_I]
        wi_ref[...] = wi[:, :H_I]
    hw = 2 * HD_B
    for h in range(H_B):
        rows_h = pl.ds(h, x.shape[0], stride=H_B)
        kb32_ref[rows_h, :] = kb[:, h * hw:(h + 1) * hw]
        vb32_ref[rows_h, :] = vb[:, h * hw:(h + 1) * hw]


def _project(x2d, w1t, b1, tm, batch=None):
    n, d = x2d.shape
    sds = jax.ShapeDtypeStruct
    row = lambda w, dt: (pl.BlockSpec((tm, w), lambda i: (i, 0)), sds((n, w), dt))
    full = lambda a: pl.BlockSpec(a.shape, lambda i: (0, 0))
    head_rows = (pl.BlockSpec((tm * H_B, 2 * HD_B), lambda i: (i, 0)),
                 sds((n * H_B, 2 * HD_B), F32))
    if batch is None:
        outs = [row(W_A, BF16), row(W_A, F32), row(W_A, F32), row(W_B, BF16), head_rows,
                head_rows, row(H_I * D_I, BF16), row(D_I, F32), row(H_I, F32)]
    else:
        t = n // batch
        nk = t // tm
        blocks = lambda w: (pl.BlockSpec((1, 1, w, tm), lambda i: (i // nk, i % nk, 0, 0)),
                            sds((batch, nk, w, tm), BF16))
        feat = lambda f, dt: (pl.BlockSpec((1, f, tm), lambda i: (i // nk, 0, i % nk)),
                              sds((batch, f, t), dt))
        outs = [feat(W_A, BF16), feat(W_A, F32), row(W_A, BF16), feat(W_A, F32), blocks(W_A),
                feat(W_B, BF16), head_rows, row(W_B, BF16), head_rows, blocks(W_B),
                feat(H_I * D_I, BF16), feat(D_I, F32), row(D_I, BF16), feat(H_I, F32)]
    return pl.pallas_call(
        functools.partial(_proj_kernel, key_major=batch is not None),
        grid=(n // tm,),
        in_specs=[pl.BlockSpec((tm, d), lambda i: (i, 0)), full(w1t), full(b1)],
        out_specs=[o[0] for o in outs],
        out_shape=[o[1] for o in outs],
        compiler_params=_cparams(("parallel",)),
        name="proj",
    )(x2d, w1t, b1)


def _float_of_key(u):
    s = u ^ jnp.int32(-2 ** 31)
    fb = jnp.where(s >= 0, s, s ^ jnp.int32(0x7FFFFFFF))
    return lax.bitcast_convert_type(fb, F32)


def _kth_largest_cell(count_ge, shape, k):
    def bit_body(i, prefix):
        bit = lax.shift_left(jnp.int32(1), jnp.int32(31) - i)
        cand = prefix | bit
        return jnp.where(count_ge(_float_of_key(cand)) >= k, cand, prefix)

    prefix = lax.fori_loop(0, 32, bit_body, jnp.zeros(shape, I32))

    def refine(_, c):
        lo, hi = c
        mid = lo + (hi - lo) * 0.5
        ok = count_ge(mid) >= k
        return jnp.where(ok, mid, lo), jnp.where(ok, hi, mid)

    return lax.fori_loop(0, TIE_REFINE_STEPS, refine,
                         (_float_of_key(prefix), _float_of_key(prefix + 1)))


def _fold_rows(x, op):
    out = x[:PART_ROWS]
    for i in range(1, x.shape[0] // PART_ROWS):
        out = op(out, x[i * PART_ROWS:(i + 1) * PART_ROWS])
    return out


def _split3(x):
    out = []
    for _ in range(3):
        part = float(np.asarray(x, np.float32).astype(BF16))
        out.append(part)
        x -= part
    return out


def _alibi_pos(tk):
    kl = lax.broadcasted_iota(I32, (tk, LANES), 0)
    col = lax.broadcasted_iota(I32, (tk, LANES), 1)
    hi = lax.shift_right_logical(kl, 6)
    lo = kl & 63
    return jnp.where(col < 3, hi, jnp.where(col < 6, lo, 0)).astype(F32).astype(BF16)


def _alibi_coef(slope, tq):
    row = lax.broadcasted_iota(I32, (LANES, tq), 0)
    coef = jnp.zeros((LANES, tq), F32)
    for i, v in enumerate(_split3(64.0 * slope) + _split3(slope)):
        coef = jnp.where(row == i, v, coef)
    return coef.astype(BF16)


def _chain_pipeline(n_chain, g, nkb, tq, acc_rows, scores, offset, values, s_scr, p_scr,
                    peel_last=False):
    n_group = n_chain // g
    neg0 = jnp.full((PART_ROWS, tq), NEG, F32)
    zero = jnp.zeros((PART_ROWS, tq), F32)
    zacc = jnp.zeros((acc_rows, tq), F32)
    ms, ls, accs = [None] * n_chain, [None] * n_chain, [None] * n_chain
    for grp in range(n_group + 2):
        def body(kb, carry, grp=grp, is_last=False):
            m_a, l_b, acc_c = list(carry[0]), list(carry[1]), list(carry[2])
            for i in range(g):
                if grp >= 2:
                    c = (grp - 2) * g + i
                    acc_c[i] = acc_c[i] + _dot(values(c, kb), p_scr[i, kb])
                if 1 <= grp <= n_group:
                    c = (grp - 1) * g + i
                    p = jnp.exp2(s_scr[i, kb] - (ms[c] - offset(c, kb)))
                    p_scr[i, kb] = p.astype(BF16)
                    l_b[i] = l_b[i] + _fold_rows(p, jnp.add)
                if grp < n_group:
                    c = grp * g + i
                    s = scores(c, kb, is_last)
                    s_scr[i, kb] = s
                    m_a[i] = jnp.maximum(m_a[i], _fold_rows(s, jnp.maximum) + offset(c, kb))
            return tuple(m_a), tuple(l_b), tuple(acc_c)

        init = ((neg0,) * g, (zero,) * g, (zacc,) * g)
        if peel_last:
            carry = lax.fori_loop(0, nkb - 1, body, init)
            m_a, l_b, acc_c = body(nkb - 1, carry, is_last=True)
        else:
            m_a, l_b, acc_c = lax.fori_loop(0, nkb, body, init)
        for i in range(g):
            if grp < n_group:
                ms[grp * g + i] = jnp.max(m_a[i], axis=0, keepdims=True)
            if 1 <= grp <= n_group:
                ls[(grp - 1) * g + i] = jnp.sum(l_b[i], axis=0, keepdims=True)
            if grp >= 2:
                accs[(grp - 2) * g + i] = acc_c[i]
    return ls, accs


def _block_geometry(qi, tq, tk):
    nkb = ((qi + 1) * tq + tk - 1) // tk
    kl = lax.broadcasted_iota(I32, (tk, tq), 0)
    ql = lax.broadcasted_iota(I32, (tk, tq), 1)

    def koff(kb):
        return kb * tk - qi * tq + jnp.zeros((1, tq), I32)

    return nkb, kl, ql, koff


def _prompt_a_kernel(qt_ref, k_ref, vt_ref, qit_ref, wt_ref, ki_ref, o_ref,
                     sc_scr, s_scr, p_scr, ot_scr, *, tq, tk, topk):
    qi = pl.program_id(1)
    nkb, kl, ql, koff = _block_geometry(qi, tq, tk)
    last = nkb - 1
    causal_last = (kl + koff(last)) <= ql
    wt = wt_ref[0]
    k = float(topk)

    def idx_body(kb, c):
        ki = ki_ref[0, pl.ds(pl.multiple_of(kb * tk, tk), tk), :]
        acc = jnp.zeros((tk, tq), F32)
        for h in range(H_I):
            r = _dot(ki, qit_ref[0, h * D_I:(h + 1) * D_I, :])
            acc = acc + jnp.maximum(r, 0.0) * wt[h:h + 1, :]
        sc_scr[kb] = acc
        return c

    lax.fori_loop(0, nkb, idx_body, 0)
    sc_scr[last] = jnp.where(causal_last, sc_scr[last], NEG)

    def count_ge(v):
        def body(kb, cnt):
            t = sc_scr[kb]
            for i in range(tk // PART_ROWS):
                cnt = jnp.where(t[i * PART_ROWS:(i + 1) * PART_ROWS] >= v, cnt + 1.0, cnt)
            return cnt
        part = lax.fori_loop(0, nkb, body, jnp.zeros((PART_ROWS, tq), F32))
        return jnp.sum(part, axis=0, keepdims=True)

    def bit_body(i, prefix):
        bit = lax.shift_left(jnp.int32(1), jnp.int32(31) - i)
        cand = prefix | bit
        return jnp.where(count_ge(_float_of_key(cand)) >= k, cand, prefix)

    prefix = lax.fori_loop(0, 32, bit_body, jnp.zeros((1, tq), I32))
    thr = _float_of_key(prefix)
    tie = jnp.max(count_ge(thr)) > k

    @pl.when(jnp.logical_not(tie))
    def _():
        def sel_body(kb, c):
            sc_scr[kb] = jnp.where(sc_scr[kb] >= thr, 0.0, NEG)
            return c
        lax.fori_loop(0, nkb, sel_body, 0)

    @pl.when(tie)
    def _():
        def refine(_, c):
            lo, hi = c
            mid = lo + (hi - lo) * 0.5
            ok = count_ge(mid) >= k
            return jnp.where(ok, mid, lo), jnp.where(ok, hi, mid)

        lo, hi = lax.fori_loop(0, TIE_REFINE_STEPS, refine, (thr, _float_of_key(prefix + 1)))
        need = k - count_ge(hi)
        below = (lax.broadcasted_iota(I32, (tk, tk), 1)
                 < lax.broadcasted_iota(I32, (tk, tk), 0)).astype(BF16)

        def sel_body(kb, seen):
            t = sc_scr[kb]
            eq = (t >= lo) & (t < hi)
            eqf = jnp.where(eq, 1.0, 0.0)
            rank = _dot(below, eqf.astype(BF16)) + seen
            sc_scr[kb] = jnp.where((t >= hi) | (eq & (rank < need)), 0.0, NEG)
            return seen + jnp.sum(eqf, axis=0, keepdims=True)

        lax.fori_loop(0, nkb, sel_body, jnp.zeros((1, tq), F32))

    sc_scr[last] = jnp.where(causal_last, sc_scr[last], NEG)

    slopes = _slopes2(H_A)
    row = lax.broadcasted_iota(I32, (LANES, tq), 0)
    pos = _alibi_pos(tk)
    qts = []
    for h in range(H_A):
        pair, odd = h // 2, h % 2
        qp = qt_ref[0, pair * LANES:(pair + 1) * LANES, :]
        qts.append(jnp.concatenate(
            [jnp.where((row >= HD_A) == bool(odd), qp, jnp.zeros_like(qp)),
             _alibi_coef(slopes[h], tq)], axis=0))

    def scores(h, kb, is_last):
        cols = slice((h // 2) * LANES, (h // 2 + 1) * LANES)
        kblk = k_ref[0, pl.ds(pl.multiple_of(kb * tk, tk), tk), cols]
        return _dot(jnp.concatenate([kblk, pos], axis=1), qts[h]) + sc_scr[kb]

    def offset(h, kb):
        return slopes[h] * koff(kb).astype(F32)

    def values(h, kb):
        return vt_ref[0, kb, h * HD_A:(h + 1) * HD_A, :]

    ls, accs = _chain_pipeline(H_A, CHAINS_PER_STAGE, nkb, tq, HD_A, scores, offset, values,
                               s_scr, p_scr)
    for h in range(H_A):
        ot_scr[h * HD_A:(h + 1) * HD_A, :] = accs[h] / ls[h]
    o_ref[0] = ot_scr[...].T.astype(o_ref.dtype)


def _prompt_a(qat, ka, vta, qit, wt, ki, tq, tk, topk):
    b, t, _ = ka.shape
    nq, nk = t // tq, t // tk
    kern = functools.partial(_prompt_a_kernel, tq=tq, tk=tk, topk=topk)
    return pl.pallas_call(
        kern,
        grid=(b, nq),
        in_specs=[
            pl.BlockSpec((1, W_A, tq), lambda i, j: (i, 0, j)),
            pl.BlockSpec((1, t, W_A), lambda i, j: (i, 0, 0)),
            pl.BlockSpec((1, nk, W_A, tk), lambda i, j: (i, 0, 0, 0)),
            pl.BlockSpec((1, H_I * D_I, tq), lambda i, j: (i, 0, j)),
            pl.BlockSpec((1, H_I, tq), lambda i, j: (i, 0, j)),
            pl.BlockSpec((1, t, D_I), lambda i, j: (i, 0, 0)),
        ],
        out_specs=pl.BlockSpec((1, tq, W_A), lambda i, j: (i, j, 0)),
        out_shape=jax.ShapeDtypeStruct((b, t, W_A), F32),
        scratch_shapes=[
            pltpu.VMEM((nk, tk, tq), F32),
            pltpu.VMEM((CHAINS_PER_STAGE, nk, tk, tq), F32),
            pltpu.VMEM((CHAINS_PER_STAGE, nk, tk, tq), BF16),
            pltpu.VMEM((W_A, tq), F32),
        ],
        compiler_params=_cparams(("parallel", "arbitrary")),
        name="prompt_a",
    )(qat, ka, vta, qit, wt, ki)


def _lambda_of(lq1_ref, lk1_ref, lq2_ref, lk2_ref, lambda_init):
    s1 = jnp.sum(lq1_ref[...] * lk1_ref[...], axis=1, keepdims=True)
    s2 = jnp.sum(lq2_ref[...] * lk2_ref[...], axis=1, keepdims=True)
    return jnp.exp(s1) - jnp.exp(s2) + lambda_init


def _prompt_b_kernel(qt_ref, k_ref, vt_ref, lq1_ref, lk1_ref, lq2_ref, lk2_ref, g_ref, o_ref,
                     s_scr, p_scr, ot_scr, *, tq, tk, lambda_init):
    qi = pl.program_id(1)
    nkb, kl, ql, koff = _block_geometry(qi, tq, tk)
    causal_last = (kl + koff(nkb - 1)) <= ql
    lam = _lambda_of(lq1_ref, lk1_ref, lq2_ref, lk2_ref, lambda_init)
    slopes = _slopes2(H_B)
    row = lax.broadcasted_iota(I32, (LANES, tq), 0)
    pos = _alibi_pos(tk)
    qts = []
    for h in range(H_B):
        qp = qt_ref[0, h * LANES:(h + 1) * LANES, :]
        coef = _alibi_coef(slopes[h], tq)
        qts.append(jnp.concatenate([jnp.where(row < HD_B, qp, jnp.zeros_like(qp)), coef], axis=0))
        qts.append(jnp.concatenate([jnp.where(row >= HD_B, qp, jnp.zeros_like(qp)), coef], axis=0))

    def scores(c, kb, is_last):
        h = c // 2
        kblk = k_ref[0, pl.ds(pl.multiple_of(kb * tk, tk), tk), h * LANES:(h + 1) * LANES]
        s = _dot(jnp.concatenate([kblk, pos], axis=1), qts[c])
        return jnp.where(causal_last, s, NEG) if is_last else s

    def offset(c, kb):
        return slopes[c // 2] * koff(kb).astype(F32)

    def values(c, kb):
        h = c // 2
        return vt_ref[0, kb, h * LANES:(h + 1) * LANES, :]

    ls, accs = _chain_pipeline(2 * H_B, CHAINS_PER_STAGE, nkb, tq, LANES, scores, offset, values,
                               s_scr, p_scr, peel_last=True)
    g = g_ref[...]
    for h in range(H_B):
        o = accs[2 * h] / ls[2 * h] - lam * (accs[2 * h + 1] / ls[2 * h + 1])
        ms = jnp.mean(o * o, axis=0, keepdims=True)
        o = o * lax.rsqrt(ms + RMS_EPS) * g
        ot_scr[h * LANES:(h + 1) * LANES, :] = o * (1.0 - lambda_init)
    o_ref[0] = ot_scr[...].T.astype(o_ref.dtype)


def _prompt_b(qbt, kb, vtb, lq1, lk1, lq2, lk2, g_col, tq, tk, lambda_init):
    b, t, _ = kb.shape
    nq, nk = t // tq, t // tk
    kern = functools.partial(_prompt_b_kernel, tq=tq, tk=tk, lambda_init=lambda_init)
    small = lambda a: pl.BlockSpec(a.shape, lambda i, j: (0, 0))
    return pl.pallas_call(
        kern,
        grid=(b, nq),
        in_specs=[
            pl.BlockSpec((1, W_B, tq), lambda i, j: (i, 0, j)),
            pl.BlockSpec((1, t, W_B), lambda i, j: (i, 0, 0)),
            pl.BlockSpec((1, nk, W_B, tk), lambda i, j: (i, 0, 0, 0)),
            small(lq1), small(lk1), small(lq2), small(lk2), small(g_col),
        ],
        out_specs=pl.BlockSpec((1, tq, W_B), lambda i, j: (i, j, 0)),
        out_shape=jax.ShapeDtypeStruct((b, t, W_B), F32),
        scratch_shapes=[
            pltpu.VMEM((CHAINS_PER_STAGE, nk, tk, tq), F32),
            pltpu.VMEM((CHAINS_PER_STAGE, nk, tk, tq), BF16),
            pltpu.VMEM((W_B, tq), F32),
        ],
        compiler_params=_cparams(("parallel", "arbitrary")),
        name="prompt_b",
    )(qbt, kb, vtb, lq1, lk1, lq2, lk2, g_col)


def _sigmoid(x):
    return 1.0 / (1.0 + jnp.exp(-x))


def _merge_kernel(x_ref, oa_ref, ob_ref, wz_ref, bz_ref, wpa_ref, wpb_ref, wout_ref,
                  lng_ref, lnb_ref, y_ref, *, alpha):
    x = x_ref[...]
    xh = x.astype(BF16)

    def proj(c0, n):
        return _dot_nt(xh, wz_ref[c0:c0 + n, :]) + bz_ref[:, c0:c0 + n]

    d = x.shape[1]
    za = proj(0, W_A)
    ua = (oa_ref[...] * (za * _sigmoid(za))).astype(BF16)
    ya = _dot(ua, wpa_ref[...])
    zb = proj(W_A, W_B)
    ub = (ob_ref[...] * (zb * _sigmoid(zb))).astype(BF16)
    yb = _dot(ub, wpb_ref[...])
    m = _sigmoid(proj(W_A + W_B, d)) * ya + _sigmoid(proj(W_A + W_B + d, d)) * yb
    r = alpha * x + _dot(m.astype(BF16), wout_ref[...])
    mu = jnp.mean(r, axis=1, keepdims=True)
    rc = r - mu
    var = jnp.mean(rc * rc, axis=1, keepdims=True)
    y_ref[...] = rc * lax.rsqrt(var + LN_EPS) * lng_ref[...] + lnb_ref[...]


def _merge(x2d, oa, ob, wz, bz, wpa, wpb, wout, lng, lnb, tm, alpha):
    n, d = x2d.shape
    row = lambda w: pl.BlockSpec((tm, w), lambda i: (i, 0))
    full = lambda a: pl.BlockSpec(a.shape, lambda i: (0, 0))
    return pl.pallas_call(
        functools.partial(_merge_kernel, alpha=alpha),
        grid=(n // tm,),
        in_specs=[row(d), row(W_A), row(W_B), full(wz), full(bz), full(wpa), full(wpb),
                  full(wout), full(lng), full(lnb)],
        out_specs=row(d),
        out_shape=jax.ShapeDtypeStruct((n, d), F32),
        compiler_params=_cparams(("parallel",)),
        name="merge",
    )(x2d, oa, ob, wz, bz, wpa, wpb, wout, lng, lnb)


def _page_specs(n, shape, first_page):
    def spec(j):
        return pl.BlockSpec((1,) + shape, lambda i, s, pt, j=j: (pt[i, first_page(s) + j], 0, 0))
    return [spec(j) for j in range(n)]


def _sample_idx_kernel(pt_ref, qi_ref, wcol_ref, kin_ref, *rest, n_pages, pps, n_tok, topk):
    kip_refs, (bias_ref, sc_scr) = rest[:pps], rest[pps:]
    step = pl.program_id(1)
    n_steps = n_pages // pps
    lane = lax.broadcasted_iota(I32, (SUBLANES, PAGE), 1)
    trow = lax.broadcasted_iota(I32, (SUBLANES, PAGE), 0)
    k = float(topk)

    def scores(kpages_t):
        r = _dot(qi_ref[0], kpages_t.astype(BF16))
        r = jnp.maximum(r, 0.0) * wcol_ref[0]
        per_tok = [jnp.sum(r[t * H_I:(t + 1) * H_I, :], axis=0, keepdims=True)
                   for t in range(n_tok)]
        pad = [jnp.zeros_like(per_tok[0])] * (SUBLANES - n_tok)
        return jnp.concatenate(per_tok + pad, axis=0)

    s_all = scores(jnp.concatenate([r[0] for r in kip_refs], axis=1))
    for j in range(pps):
        sc_scr[step * pps + j] = s_all[:, j * PAGE:(j + 1) * PAGE]

    @pl.when(step == n_steps - 1)
    def _():
        new_ok = (lane <= trow) & (lane < n_tok)
        s = jnp.where(lane <= trow, scores(kin_ref[0]), NEG)
        sc_scr[n_pages] = jnp.where(lane < n_tok, s, PAD_SCORE)

        def count_ge(v):
            parts = [jnp.zeros((SUBLANES, PAGE), F32)] * 4
            for j in range(n_pages + 1):
                parts[j % 4] = parts[j % 4] + jnp.where(sc_scr[j] >= v, 1.0, 0.0)
            return jnp.sum((parts[0] + parts[1]) + (parts[2] + parts[3]), axis=1, keepdims=True)

        lo, hi = _kth_largest_cell(count_ge, (SUBLANES, 1), k)
        need = k - count_ge(hi)
        before = (lax.broadcasted_iota(I32, (PAGE, PAGE), 0)
                  < lax.broadcasted_iota(I32, (PAGE, PAGE), 1)).astype(BF16)
        seen = jnp.zeros((SUBLANES, 1), F32)
        for j in range(n_pages + 1):
            t = sc_scr[j]
            eq = (t >= lo) & (t < hi)
            eqf = jnp.where(eq, 1.0, 0.0)
            rank = _dot(eqf.astype(BF16), before) + seen
            sel = (t >= hi) | (eq & (rank < need))
            if j == n_pages:
                sel = sel & new_ok
            bias_ref[0, j] = jnp.where(sel, 0.0, NEG)
            seen = seen + jnp.sum(eqf, axis=1, keepdims=True)


def _sample_idx(page_table, qi_rows, wcol, cache_ki_t, ki_new_t, n_tok, topk):
    b, n_pages = page_table.shape
    rows = n_tok * H_I
    pps = min(IDX_PAGES_PER_STEP, n_pages)
    kern = functools.partial(_sample_idx_kernel, n_pages=n_pages, pps=pps, n_tok=n_tok,
                             topk=topk)
    grid_spec = pltpu.PrefetchScalarGridSpec(
        num_scalar_prefetch=1,
        grid=(b, n_pages // pps),
        in_specs=[
            pl.BlockSpec((1, rows, D_I), lambda i, s, pt: (i, 0, 0)),
            pl.BlockSpec((1, rows, 1), lambda i, s, pt: (i, 0, 0)),
            pl.BlockSpec((1, D_I, PAGE), lambda i, s, pt: (i, 0, 0)),
        ] + _page_specs(pps, (D_I, PAGE), lambda s: s * pps),
        out_specs=pl.BlockSpec((1, n_pages + 1, SUBLANES, PAGE), lambda i, s, pt: (i, 0, 0, 0)),
        scratch_shapes=[pltpu.VMEM((n_pages + 1, SUBLANES, PAGE), F32)],
    )
    return pl.pallas_call(
        kern,
        grid_spec=grid_spec,
        out_shape=jax.ShapeDtypeStruct((b, n_pages + 1, SUBLANES, PAGE), F32),
        compiler_params=_cparams(("parallel", "arbitrary")),
        name="sample_idx",
    )(page_table, qi_rows, wcol, ki_new_t, *([cache_ki_t] * pps))


def _softmax_update(s, m_scr, l_scr):
    m = m_scr[...]
    m_new = jnp.maximum(m, jnp.max(s, axis=1, keepdims=True))
    p = jnp.exp2(s - m_new)
    alpha = jnp.exp2(m - m_new)
    l_scr[...] = alpha * l_scr[...] + jnp.sum(p, axis=1, keepdims=True)
    m_scr[...] = m_new
    return p, alpha


def _decode_a_kernel(pt_ref, wq_ref, slope_ref, kn_ref, vn_ref, bias_ref, biasn_ref, *rest,
                     n_pages, pps, n_tok, past_len):
    kp_refs, vp_refs = rest[:pps], rest[pps:2 * pps]
    o_ref, acc_scr, m_scr, l_scr = rest[2 * pps:]
    step = pl.program_id(1)
    n_steps = n_pages // pps
    rows = n_tok * H_A

    @pl.when(step == 0)
    def _():
        acc_scr[...] = jnp.zeros_like(acc_scr)
        m_scr[...] = jnp.full_like(m_scr, NEG)
        l_scr[...] = jnp.zeros_like(l_scr)

    def update(k_ts, v_ts, first_pos, b8s, new_keys):
        n = len(k_ts)
        lane = lax.broadcasted_iota(I32, (rows, n * PAGE), 1)
        trow = lax.broadcasted_iota(I32, (rows, n * PAGE), 0) // H_A
        s = jnp.concatenate([_dot(wq_ref[0], k_t.astype(BF16)) for k_t in k_ts], axis=1)
        s = s - slope_ref[...] * (past_len + trow - (first_pos + lane)).astype(F32)
        s = s + jnp.concatenate(
            [jnp.broadcast_to(jnp.concatenate([b8[t:t + 1, :] for b8 in b8s], axis=1),
                              (H_A, n * PAGE)) for t in range(n_tok)], axis=0)
        if new_keys:
            s = jnp.where((lane <= trow) & (lane < n_tok), s, NEG)
        p, alpha = _softmax_update(s, m_scr, l_scr)
        p = p.astype(BF16)
        pv = _dot_nt(p[:, :PAGE], v_ts[0].astype(BF16))
        for j in range(1, n):
            pv = pv + _dot_nt(p[:, j * PAGE:(j + 1) * PAGE], v_ts[j].astype(BF16))
        acc_scr[...] = acc_scr[...] * alpha + pv

    update([r[0] for r in kp_refs], [r[0] for r in vp_refs], step * (pps * PAGE),
           [bias_ref[0, j] for j in range(pps)], False)

    @pl.when(step == n_steps - 1)
    def _():
        update([kn_ref[0]], [vn_ref[0]], past_len, [biasn_ref[0, 0]], True)
        o = acc_scr[...] / l_scr[...]
        own = (lax.broadcasted_iota(I32, (rows, W_A), 1) // HD_A
               == lax.broadcasted_iota(I32, (rows, W_A), 0) % H_A)
        o = jnp.where(own, o, 0.0)
        o_ref[0] = jnp.concatenate(
            [jnp.sum(o[t * H_A:(t + 1) * H_A, :], axis=0, keepdims=True) for t in range(n_tok)],
            axis=0)


def _decode_a(page_table, wq, slope_col, cache_k_t, cache_v_t, k_new_t, v_new_t, bias, n_tok,
              past_len):
    b, n_pages = page_table.shape
    rows = n_tok * H_A
    pps = min(PAGES_PER_STEP, n_pages)
    kern = functools.partial(_decode_a_kernel, n_pages=n_pages, pps=pps, n_tok=n_tok,
                             past_len=past_len)
    pages = _page_specs(pps, (W_A, PAGE), lambda s: s * pps)
    per_batch = lambda shape: pl.BlockSpec((1,) + shape, lambda i, s, pt: (i,) + (0,) * len(shape))
    grid_spec = pltpu.PrefetchScalarGridSpec(
        num_scalar_prefetch=1,
        grid=(b, n_pages // pps),
        in_specs=[
            per_batch((rows, W_A)),
            pl.BlockSpec((rows, 1), lambda i, s, pt: (0, 0)),
            per_batch((W_A, PAGE)), per_batch((W_A, PAGE)),
            pl.BlockSpec((1, pps, SUBLANES, PAGE), lambda i, s, pt: (i, s, 0, 0)),
            pl.BlockSpec((1, 1, SUBLANES, PAGE), lambda i, s, pt: (i, n_pages, 0, 0)),
        ] + pages + pages,
        out_specs=per_batch((n_tok, W_A)),
        scratch_shapes=[pltpu.VMEM((rows, W_A), F32), pltpu.VMEM((rows, 1), F32),
                        pltpu.VMEM((rows, 1), F32)],
    )
    return pl.pallas_call(
        kern,
        grid_spec=grid_spec,
        out_shape=jax.ShapeDtypeStruct((b, n_tok, W_A), F32),
        compiler_params=_cparams(("parallel", "arbitrary")),
        name="decode_a",
    )(page_table, wq, slope_col, k_new_t, v_new_t, bias, bias,
      *([cache_k_t] * pps), *([cache_v_t] * pps))


def _decode_b_kernel(pt_ref, wq_ref, kn_ref, vn_ref, lq1_ref, lk1_ref, lq2_ref, lk2_ref, g_ref,
                     *rest, n_pages, pps, n_tok, past_len, lambda_init):
    kp_refs, vp_refs = rest[:pps], rest[pps:2 * pps]
    o_ref, acc_scr, m_scr, l_scr = rest[2 * pps:]
    step = pl.program_id(1)
    n_steps = n_pages // pps
    hrows = 2 * n_tok
    rows = H_B * hrows
    slope_col = jnp.concatenate([jnp.full((hrows, 1), s, F32) for s in _slopes2(H_B)], axis=0)

    @pl.when(step == 0)
    def _():
        acc_scr[...] = jnp.zeros_like(acc_scr)
        m_scr[...] = jnp.full_like(m_scr, NEG)
        l_scr[...] = jnp.zeros_like(l_scr)

    def update(k_hs, v_hs, first_pos, new_keys):
        n = len(k_hs[0])
        lane = lax.broadcasted_iota(I32, (rows, n * PAGE), 1)
        trow = lax.broadcasted_iota(I32, (rows, n * PAGE), 0) % n_tok
        s = jnp.concatenate(
            [jnp.concatenate([_dot_nt(wq_ref[0, h], k.astype(BF16)) for k in k_hs[h]], axis=1)
             for h in range(H_B)], axis=0)
        s = s - slope_col * (past_len + trow - (first_pos + lane)).astype(F32)
        if new_keys:
            s = jnp.where((lane <= trow) & (lane < n_tok), s, NEG)
        p, alpha = _softmax_update(s, m_scr, l_scr)
        p = p.astype(BF16)
        pvs = []
        for h in range(H_B):
            ph = p[h * hrows:(h + 1) * hrows, :]
            pv = _dot(ph[:, :PAGE], v_hs[h][0].astype(BF16))
            for j in range(1, n):
                pv = pv + _dot(ph[:, j * PAGE:(j + 1) * PAGE], v_hs[h][j].astype(BF16))
            pvs.append(pv)
        acc_scr[...] = acc_scr[...] * alpha + jnp.concatenate(pvs, axis=0)

    head_rows = lambda refs, h: [r[0, pl.ds(h, PAGE, stride=H_B), :] for r in refs]
    update([head_rows(kp_refs, h) for h in range(H_B)],
           [head_rows(vp_refs, h) for h in range(H_B)], step * (pps * PAGE), False)

    @pl.when(step == n_steps - 1)
    def _():
        update([[kn_ref[0, h]] for h in range(H_B)], [[vn_ref[0, h]] for h in range(H_B)],
               past_len, True)
        lam = _lambda_of(lq1_ref, lk1_ref, lq2_ref, lk2_ref, lambda_init)
        o_all = acc_scr[...] / l_scr[...]
        for h in range(H_B):
            o = (o_all[h * hrows:h * hrows + n_tok, :]
                 - lam * o_all[h * hrows + n_tok:(h + 1) * hrows, :])
            ms = jnp.mean(o * o, axis=1, keepdims=True)
            o = o * lax.rsqrt(ms + RMS_EPS) * g_ref[...] * (1.0 - lambda_init)
            o_ref[0, :, h * LANES:(h + 1) * LANES] = o


def _decode_b(page_table, wq, cache_k_r, cache_v_r, k_new, v_new, lam_args, g_row, n_tok,
              past_len, lambda_init):
    b, n_pages = page_table.shape
    rows = 2 * n_tok
    pps = min(PAGES_PER_STEP, n_pages)
    kern = functools.partial(_decode_b_kernel, n_pages=n_pages, pps=pps, n_tok=n_tok,
                             past_len=past_len, lambda_init=lambda_init)
    pages = _page_specs(pps, (PAGE * H_B, 2 * HD_B), lambda s: s * pps)
    per_batch = lambda shape: pl.BlockSpec((1,) + shape, lambda i, s, pt: (i,) + (0,) * len(shape))
    small = lambda a: pl.BlockSpec(a.shape, lambda i, s, pt: (0, 0))
    grid_spec = pltpu.PrefetchScalarGridSpec(
        num_scalar_prefetch=1,
        grid=(b, n_pages // pps),
        in_specs=[
            per_batch((H_B, rows, 2 * HD_B)),
            per_batch((H_B, PAGE, 2 * HD_B)), per_batch((H_B, PAGE, 2 * HD_B)),
        ] + [small(a) for a in lam_args] + [small(g_row)] + pages + pages,
        out_specs=per_batch((n_tok, W_B)),
        scratch_shapes=[pltpu.VMEM((H_B * rows, 2 * HD_B), F32), pltpu.VMEM((H_B * rows, 1), F32),
                        pltpu.VMEM((H_B * rows, 1), F32)],
    )
    return pl.pallas_call(
        kern,
        grid_spec=grid_spec,
        out_shape=jax.ShapeDtypeStruct((b, n_tok, W_B), F32),
        compiler_params=_cparams(("parallel", "arbitrary")),
        name="decode_b",
    )(page_table, wq, k_new, v_new, *lam_args, g_row,
      *([cache_k_r] * pps), *([cache_v_r] * pps))


def _split_rows(wt, d_model):
    out, off = {}, 0
    for name, n in SPLITS:
        n = d_model if n is None else n
        out[name] = wt[off:off + n]
        off += n
    return out


def _pad_rows(a, rows):
    return jnp.pad(a, [(0, rows - a.shape[0])] + [(0, 0)] * (a.ndim - 1))


def _proj_weights(w_in, b_in, d_model):
    w = _split_rows(w_in.T, d_model)
    b = _split_rows(b_in, d_model)
    order1 = ['q_a', 'k_a', 'v_a', 'q_b', 'k_b', 'v_b', 'q_idx']
    group1 = lambda p: jnp.concatenate(
        [p[n] for n in order1] + [_pad_rows(p['k_idx'], LANES), _pad_rows(p['w_idx'], LANES)])
    order2 = ['z_a', 'z_b', 'g_a', 'g_b']
    group2 = lambda p: jnp.concatenate([p[n] for n in order2])
    return (group1(w).astype(BF16), group1(b)[None, :], group2(w).astype(BF16),
            group2(b)[None, :])


def _prompt_branches(xp, bp, t, w1t, b1, lam_args, g, tq, tk, topk, lambda_init):
    (qat, kat32, ka16, vat32, vat, qbt, kb32, kb16, vb32, vbt, qit, kit32, ki16, wit) = _project(
        xp, w1t, b1, tk, batch=bp)
    r3 = lambda a: a.reshape(bp, t, a.shape[-1])
    o_a = _prompt_a(qat, r3(ka16), vat, qit, wit, r3(ki16), tq, tk, topk)
    o_b = _prompt_b(qbt, r3(kb16), vbt, *lam_args, g[:, None], tq, tk, lambda_init)
    kv = (kat32.reshape(bp, H_A, HD_A, t).transpose(0, 3, 1, 2),
          vat32.reshape(bp, H_A, HD_A, t).transpose(0, 3, 1, 2),
          kit32.transpose(0, 2, 1),
          kb32.reshape(bp, t, H_B, 2 * HD_B), vb32.reshape(bp, t, H_B, 2 * HD_B))
    return o_a.reshape(bp * t, W_A), o_b.reshape(bp * t, W_B), kv


def _sample_branches(xs, bs, ts, page_table, c_k_a, c_v_a, c_k_idx, c_k_b, c_v_b, w1t, b1,
                     lam_args, g, topk, lambda_init):
    n_pool = c_k_a.shape[0]
    past_len = page_table.shape[1] * PAGE
    qa, ka32, va32, qb, kb32, vb32, qidx, ki32, wi = _project(xs, w1t, b1, bs * ts)
    s3 = lambda a: a.reshape(bs, ts, a.shape[-1])
    new_t = lambda a: jnp.pad(s3(a).transpose(0, 2, 1), ((0, 0), (0, 0), (0, PAGE - ts)))
    qi_rows = s3(qidx).reshape(bs, ts * H_I, D_I)
    wcol = s3(wi).reshape(bs, ts * H_I, 1)
    bias = _sample_idx(page_table, qi_rows, wcol, c_k_idx.transpose(0, 2, 1), new_t(ki32), ts,
                       topk)
    eye_a = jnp.eye(H_A, dtype=BF16)
    wq_a = (s3(qa).reshape(bs, ts, H_A, 1, HD_A) * eye_a[None, None, :, :, None]
            ).reshape(bs, ts * H_A, W_A)
    slope_a = jnp.tile(jnp.asarray(_slopes2(H_A), F32), ts)[:, None]
    page_t = lambda c: c.transpose(0, 2, 3, 1).reshape(n_pool, W_A, PAGE)
    o_a = _decode_a(page_table, wq_a, slope_a, page_t(c_k_a), page_t(c_v_a), new_t(ka32),
                    new_t(va32), bias, ts, past_len)
    qb4 = s3(qb).reshape(bs, ts, H_B, 2 * HD_B).transpose(0, 2, 1, 3)
    first = (jnp.arange(2 * HD_B) < HD_B)
    wq_b = jnp.concatenate([jnp.where(first, qb4, 0), jnp.where(first, 0, qb4)], axis=2)
    new_h = lambda a: jnp.pad(a.reshape(bs, ts, H_B, 2 * HD_B).transpose(0, 2, 1, 3),
                              ((0, 0), (0, 0), (0, PAGE - ts), (0, 0)))
    page_r = lambda c: c.reshape(n_pool, PAGE * H_B, 2 * HD_B)
    o_b = _decode_b(page_table, wq_b, page_r(c_k_b), page_r(c_v_b), new_h(kb32), new_h(vb32),
                    lam_args, g[None, :], ts, past_len, lambda_init)
    return (o_a.reshape(bs * ts, W_A), o_b.reshape(bs * ts, W_B),
            (ka32, va32, ki32, kb32, vb32))


def kernel(x_prompt, x_sample, cache_k_a, cache_v_a, cache_k_idx, cache_k_b, cache_v_b, page_table,
           w_in, b_in, lambda_q1, lambda_k1, lambda_q2, lambda_k2, subln_g, w_pa, w_pb, w_out,
           ln_g, ln_b):
    depth = w_in.shape[0]
    bp, t, d = x_prompt.shape
    bs, ts, _ = x_sample.shape
    past_len = page_table.shape[1] * PAGE
    alpha = (2.0 * depth) ** 0.25
    tq = min(256, t)
    tk = min(512, t)
    tm = min(512, bp * t)
    topk_p = min(TOPK_MAX, t // 4)
    topk_s = min(TOPK_MAX, (past_len + ts) // 4)

    xp = x_prompt.reshape(bp * t, d)
    xs = x_sample.reshape(bs * ts, d)
    pk, sk = [], []
    for l in range(depth):
        lambda_init = 0.8 - 0.6 * math.exp(-0.3 * l)
        w1t, b1, wzt, bz = _proj_weights(w_in[l], b_in[l], d)
        lam_args = (lambda_q1[l][None, :], lambda_k1[l][None, :],
                    lambda_q2[l][None, :], lambda_k2[l][None, :])
        merge_w = (wzt, bz, w_pa[l].astype(BF16), w_pb[l].astype(BF16), w_out[l].astype(BF16),
                   ln_g[l][None, :], ln_b[l][None, :])
        o_a, o_b, kv = _prompt_branches(xp, bp, t, w1t, b1, lam_args, subln_g[l], tq, tk, topk_p,
                                        lambda_init)
        pk.append(kv)
        xp = _merge(xp, o_a, o_b, *merge_w, tm, alpha)
        o_a, o_b, kv = _sample_branches(xs, bs, ts, page_table, cache_k_a[l], cache_v_a[l],
                                        cache_k_idx[l], cache_k_b[l], cache_v_b[l], w1t, b1,
                                        lam_args, subln_g[l], topk_s, lambda_init)
        sk.append(kv)
        xs = _merge(xs, o_a, o_b, *merge_w, bs * ts, alpha)

    def stacked(layers, b, n):
        shapes = ((b, n, H_A, HD_A), (b, n, H_A, HD_A), (b, n, D_I),
                  (b, n, H_B, 2 * HD_B), (b, n, H_B, 2 * HD_B))
        return tuple(jnp.stack([kv[i].reshape(s) for kv in layers]) for i, s in enumerate(shapes))

    return (xp.reshape(bp, t, d), xs.reshape(bs, ts, d)) + stacked(pk, bp, t) + stacked(sk, bs, ts)
```

```python
import functools
import math

import jax
import jax.numpy as jnp
import numpy as np
from jax import lax
from jax.experimental import pallas as pl
from jax.experimental.pallas import tpu as pltpu

F32 = jnp.float32
BF16 = jnp.bfloat16
I32 = jnp.int32

H_A, HD_A = 8, 64
H_I, D_I = 8, 32
H_B, HD_B = 4, 64
TOPK_MAX = 256
PAGE = 128
LN_EPS = 1e-5
RMS_EPS = 1e-5
NEG = -1e30
PAD_SCORE = -3e38
LOG2E = math.log2(math.e)

W_A = H_A * HD_A
W_B = H_B * 2 * HD_B
LANES = 128
SUBLANES = 8
VMEM_LIMIT = 56 * 1024 * 1024
TIE_REFINE_STEPS = 16
PAGES_PER_STEP = 16
IDX_PAGES_PER_STEP = 64
CHAINS_PER_STAGE = 2
PART_ROWS = 64

SPLITS = (
    ('q_a', W_A), ('k_a', W_A), ('v_a', W_A),
    ('q_idx', H_I * D_I), ('k_idx', D_I), ('w_idx', H_I), ('z_a', W_A),
    ('q_b', W_B), ('k_b', W_B), ('v_b', W_B), ('z_b', W_B),
    ('g_a', None), ('g_b', None),
)


def _slopes2(n):
    return tuple(LOG2E * 2.0 ** (-8.0 * (i + 1) / n) for i in range(n))


def _cparams(sem):
    return pltpu.CompilerParams(dimension_semantics=sem, vmem_limit_bytes=VMEM_LIMIT)


def _dot_nt(a, b):
    return lax.dot_general(a, b, (((1,), (1,)), ((), ())), preferred_element_type=F32)


def _dot(a, b):
    return jnp.dot(a, b, preferred_element_type=F32)


_C_QA, _C_KA, _C_VA, _C_QB, _C_KB, _C_VB = 0, 512, 1024, 1536, 2048, 2560
_C_QI, _C_KI, _C_WI, _C_END = 3072, 3328, 3456, 3584


def _proj_kernel(x_ref, w_ref, b_ref, *out_refs, key_major):
    x = x_ref[...].astype(BF16)

    def proj(c0, n):
        return _dot_nt(x, w_ref[c0:c0 + n, :]) + b_ref[:, c0:c0 + n]

    qa = proj(_C_QA, W_A) * (LOG2E * HD_A ** -0.5)
    ka, va = proj(_C_KA, W_A), proj(_C_VA, W_A)
    qb = proj(_C_QB, W_B) * (LOG2E * HD_B ** -0.5)
    kb, vb = proj(_C_KB, W_B), proj(_C_VB, W_B)
    qi = proj(_C_QI, H_I * D_I)
    ki = proj(_C_KI, LANES)
    wi = proj(_C_WI, LANES) * ((H_I ** -0.5) * (D_I ** -0.5))
    if key_major:
        (qat_ref, kat32_ref, ka16_ref, vat32_ref, vat_ref, qbt_ref, kb32_ref, kb16_ref, vb32_ref,
         vbt_ref, qit_ref, kit32_ref, ki16_ref, wit_ref) = out_refs
        qat_ref[0] = qa.T.astype(BF16)
        qbt_ref[0] = qb.T.astype(BF16)
        kat32_ref[0] = ka.T
        ka16_ref[...] = ka.astype(BF16)
        kb16_ref[...] = kb.astype(BF16)
        va_t, vb_t = va.T, vb.T
        vat32_ref[0] = va_t
        vat_ref[0, 0] = va_t.astype(BF16)
        vbt_ref[0, 0] = vb_t.astype(BF16)
        qit_ref[0] = qi.T.astype(BF16)
        kit32_ref[0] = ki.T[:D_I, :]
        ki16_ref[...] = ki[:, :D_I].astype(BF16)
        wit_ref[0] = wi.T[:H_I, :]
    else:
        qa_ref, ka32_ref, va32_ref, qb_ref, kb32_ref, vb32_ref, qi_ref, ki32_ref, wi_ref = out_refs
        qa_ref[...] = qa.astype(BF16)
        qb_ref[...] = qb.astype(BF16)
        ka32_ref[...] = ka
        va32_ref[...] = va
        qi_ref[...] = qi.astype(BF16)
        ki32_ref[...] = ki[:, :D_I]
        wi_ref[...] = wi[:, :H_I]
    hw = 2 * HD_B
    for h in range(H_B):
        rows_h = pl.ds(h, x.shape[0], stride=H_B)
        kb32_ref[rows_h, :] = kb[:, h * hw:(h + 1) * hw]
        vb32_ref[rows_h, :] = vb[:, h * hw:(h + 1) * hw]


def _project(x2d, w1t, b1, tm, batch=None):
    n, d = x2d.shape
    sds = jax.ShapeDtypeStruct
    row = lambda w, dt: (pl.BlockSpec((tm, w), lambda i: (i, 0)), sds((n, w), dt))
    full = lambda a: pl.BlockSpec(a.shape, lambda i: (0, 0))
    head_rows = (pl.BlockSpec((tm * H_B, 2 * HD_B), lambda i: (i, 0)),
                 sds((n * H_B, 2 * HD_B), F32))
    if batch is None:
        outs = [row(W_A, BF16), row(W_A, F32), row(W_A, F32), row(W_B, BF16), head_rows,
                head_rows, row(H_I * D_I, BF16), row(D_I, F32), row(H_I, F32)]
    else:
        t = n // batch
        nk = t // tm
        blocks = lambda w: (pl.BlockSpec((1, 1, w, tm), lambda i: (i // nk, i % nk, 0, 0)),
                            sds((batch, nk, w, tm), BF16))
        feat = lambda f, dt: (pl.BlockSpec((1, f, tm), lambda i: (i // nk, 0, i % nk)),
                              sds((batch, f, t), dt))
        outs = [feat(W_A, BF16), feat(W_A, F32), row(W_A, BF16), feat(W_A, F32), blocks(W_A),
                feat(W_B, BF16), head_rows, row(W_B, BF16), head_rows, blocks(W_B),
                feat(H_I * D_I, BF16), feat(D_I, F32), row(D_I, BF16), feat(H_I, F32)]
    return pl.pallas_call(
        functools.partial(_proj_kernel, key_major=batch is not None),
        grid=(n // tm,),
        in_specs=[pl.BlockSpec((tm, d), lambda i: (i, 0)), full(w1t), full(b1)],
        out_specs=[o[0] for o in outs],
        out_shape=[o[1] for o in outs],
        compiler_params=_cparams(("parallel",)),
        name="proj",
    )(x2d, w1t, b1)


def _float_of_key(u):
    s = u ^ jnp.int32(-2 ** 31)
    fb = jnp.where(s >= 0, s, s ^ jnp.int32(0x7FFFFFFF))
    return lax.bitcast_convert_type(fb, F32)


def _kth_largest_cell(count_ge, shape, k):
    def bit_body(i, prefix):
        bit = lax.shift_left(jnp.int32(1), jnp.int32(31) - i)
        cand = prefix | bit
        return jnp.where(count_ge(_float_of_key(cand)) >= k, cand, prefix)

    prefix = lax.fori_loop(0, 32, bit_body, jnp.zeros(shape, I32))

    def refine(_, c):
        lo, hi = c
        mid = lo + (hi - lo) * 0.5
        ok = count_ge(mid) >= k
        return jnp.where(ok, mid, lo), jnp.where(ok, hi, mid)

    return lax.fori_loop(0, TIE_REFINE_STEPS, refine,
                         (_float_of_key(prefix), _float_of_key(prefix + 1)))


def _fold_rows(x, op):
    out = x[:PART_ROWS]
    for i in range(1, x.shape[0] // PART_ROWS):
        out = op(out, x[i * PART_ROWS:(i + 1) * PART_ROWS])
    return out


def _split3(x):
    out = []
    for _ in range(3):
        part = float(np.asarray(x, np.float32).astype(BF16))
        out.append(part)
        x -= part
    return out


def _alibi_pos(tk):
    kl = lax.broadcasted_iota(I32, (tk, LANES), 0)
    col = lax.broadcasted_iota(I32, (tk, LANES), 1)
    hi = lax.shift_right_logical(kl, 6)
    lo = kl & 63
    return jnp.where(col < 3, hi, jnp.where(col < 6, lo, 0)).astype(F32).astype(BF16)


def _alibi_coef(slope, tq):
    row = lax.broadcasted_iota(I32, (LANES, tq), 0)
    coef = jnp.zeros((LANES, tq), F32)
    for i, v in enumerate(_split3(64.0 * slope) + _split3(slope)):
        coef = jnp.where(row == i, v, coef)
    return coef.astype(BF16)


def _chain_pipeline(n_chain, g, nkb, tq, acc_rows, scores, offset, values, s_scr, p_scr,
                    peel_last=False):
    n_group = n_chain // g
    neg0 = jnp.full((PART_ROWS, tq), NEG, F32)
    zero = jnp.zeros((PART_ROWS, tq), F32)
    zacc = jnp.zeros((acc_rows, tq), F32)
    ms, ls, accs = [None] * n_chain, [None] * n_chain, [None] * n_chain
    for grp in range(n_group + 2):
        def body(kb, carry, grp=grp, is_last=False):
            m_a, l_b, acc_c = list(carry[0]), list(carry[1]), list(carry[2])
            for i in range(g):
                if grp >= 2:
                    c = (grp - 2) * g + i
                    acc_c[i] = acc_c[i] + _dot(values(c, kb), p_scr[i, kb])
                if 1 <= grp <= n_group:
                    c = (grp - 1) * g + i
                    p = jnp.exp2(s_scr[i, kb] - (ms[c] - offset(c, kb)))
                    p_scr[i, kb] = p.astype(BF16)
                    l_b[i] = l_b[i] + _fold_rows(p, jnp.add)
                if grp < n_group:
                    c = grp * g + i
                    s = scores(c, kb, is_last)
                    s_scr[i, kb] = s
                    m_a[i] = jnp.maximum(m_a[i], _fold_rows(s, jnp.maximum) + offset(c, kb))
            return tuple(m_a), tuple(l_b), tuple(acc_c)

        n_loop = nkb - 1 if peel_last else nkb
        carry = lax.fori_loop(0, n_loop // 2,
                              lambda j, c, body=body: body(2 * j + 1, body(2 * j, c)),
                              ((neg0,) * g, (zero,) * g, (zacc,) * g))
        carry = lax.cond(n_loop % 2 == 1, lambda c, body=body: body(n_loop - 1, c),
                         lambda c: c, carry)
        m_a, l_b, acc_c = body(nkb - 1, carry, is_last=True) if peel_last else carry
        for i in range(g):
            if grp < n_group:
                ms[grp * g + i] = jnp.max(m_a[i], axis=0, keepdims=True)
            if 1 <= grp <= n_group:
                ls[(grp - 1) * g + i] = jnp.sum(l_b[i], axis=0, keepdims=True)
            if grp >= 2:
                accs[(grp - 2) * g + i] = acc_c[i]
    return ls, accs


def _block_geometry(qi, tq, tk):
    nkb = ((qi + 1) * tq + tk - 1) // tk
    kl = lax.broadcasted_iota(I32, (tk, tq), 0)
    ql = lax.broadcasted_iota(I32, (tk, tq), 1)

    def koff(kb):
        return kb * tk - qi * tq + jnp.zeros((1, tq), I32)

    return nkb, kl, ql, koff


def _prompt_a_kernel(qt_ref, k_ref, vt_ref, qit_ref, wt_ref, ki_ref, o_ref,
                     sc_scr, s_scr, p_scr, ot_scr, *, tq, tk, topk):
    qi = pl.program_id(1)
    nkb, kl, ql, koff = _block_geometry(qi, tq, tk)
    last = nkb - 1
    causal_last = (kl + koff(last)) <= ql
    wt = wt_ref[0]
    k = float(topk)

    def idx_body(kb, c):
        ki = ki_ref[0, pl.ds(pl.multiple_of(kb * tk, tk), tk), :]
        acc = jnp.zeros((tk, tq), F32)
        for h in range(H_I):
            r = _dot(ki, qit_ref[0, h * D_I:(h + 1) * D_I, :])
            acc = acc + jnp.maximum(r, 0.0) * wt[h:h + 1, :]
        sc_scr[kb] = acc
        return c

    lax.fori_loop(0, nkb, idx_body, 0)
    sc_scr[last] = jnp.where(causal_last, sc_scr[last], NEG)

    def count_ge(v):
        def body(kb, cnt):
            t = sc_scr[kb]
            for i in range(tk // PART_ROWS):
                cnt = jnp.where(t[i * PART_ROWS:(i + 1) * PART_ROWS] >= v, cnt + 1.0, cnt)
            return cnt
        part = lax.fori_loop(0, nkb, body, jnp.zeros((PART_ROWS, tq), F32))
        return jnp.sum(part, axis=0, keepdims=True)

    def bit_body(i, prefix):
        bit = lax.shift_left(jnp.int32(1), jnp.int32(31) - i)
        cand = prefix | bit
        return jnp.where(count_ge(_float_of_key(cand)) >= k, cand, prefix)

    prefix = lax.fori_loop(0, 32, bit_body, jnp.zeros((1, tq), I32))
    thr = _float_of_key(prefix)
    tie = jnp.max(count_ge(thr)) > k

    @pl.when(jnp.logical_not(tie))
    def _():
        def sel_body(kb, c):
            sc_scr[kb] = jnp.where(sc_scr[kb] >= thr, 0.0, NEG)
            return c
        lax.fori_loop(0, nkb, sel_body, 0)

    @pl.when(tie)
    def _():
        def refine(_, c):
            lo, hi = c
            mid = lo + (hi - lo) * 0.5
            ok = count_ge(mid) >= k
            return jnp.where(ok, mid, lo), jnp.where(ok, hi, mid)

        lo, hi = lax.fori_loop(0, TIE_REFINE_STEPS, refine, (thr, _float_of_key(prefix + 1)))
        need = k - count_ge(hi)
        below = (lax.broadcasted_iota(I32, (tk, tk), 1)
                 < lax.broadcasted_iota(I32, (tk, tk), 0)).astype(BF16)

        def sel_body(kb, seen):
            t = sc_scr[kb]
            eq = (t >= lo) & (t < hi)
            eqf = jnp.where(eq, 1.0, 0.0)
            rank = _dot(below, eqf.astype(BF16)) + seen
            sc_scr[kb] = jnp.where((t >= hi) | (eq & (rank < need)), 0.0, NEG)
            return seen + jnp.sum(eqf, axis=0, keepdims=True)

        lax.fori_loop(0, nkb, sel_body, jnp.zeros((1, tq), F32))

    sc_scr[last] = jnp.where(causal_last, sc_scr[last], NEG)

    slopes = _slopes2(H_A)
    row = lax.broadcasted_iota(I32, (LANES, tq), 0)
    pos = _alibi_pos(tk)
    qts = []
    for h in range(H_A):
        pair, odd = h // 2, h % 2
        qp = qt_ref[0, pair * LANES:(pair + 1) * LANES, :]
        qts.append(jnp.concatenate(
            [jnp.where((row >= HD_A) == bool(odd), qp, jnp.zeros_like(qp)),
             _alibi_coef(slopes[h], tq)], axis=0))

    def scores(h, kb, is_last):
        cols = slice((h // 2) * LANES, (h // 2 + 1) * LANES)
        kblk = k_ref[0, pl.ds(pl.multiple_of(kb * tk, tk), tk), cols]
        return _dot(jnp.concatenate([kblk, pos], axis=1), qts[h]) + sc_scr[kb]

    def offset(h, kb):
        return slopes[h] * koff(kb).astype(F32)

    def values(h, kb):
        return vt_ref[0, kb, h * HD_A:(h + 1) * HD_A, :]

    ls, accs = _chain_pipeline(H_A, CHAINS_PER_STAGE, nkb, tq, HD_A, scores, offset, values,
                               s_scr, p_scr)
    for h in range(H_A):
        ot_scr[h * HD_A:(h + 1) * HD_A, :] = accs[h] / ls[h]
    o_ref[0] = ot_scr[...].T.astype(o_ref.dtype)


def _prompt_a(qat, ka, vta, qit, wt, ki, tq, tk, topk):
    b, t, _ = ka.shape
    nq, nk = t // tq, t // tk
    kern = functools.partial(_prompt_a_kernel, tq=tq, tk=tk, topk=topk)
    return pl.pallas_call(
        kern,
        grid=(b, nq),
        in_specs=[
            pl.BlockSpec((1, W_A, tq), lambda i, j: (i, 0, j)),
            pl.BlockSpec((1, t, W_A), lambda i, j: (i, 0, 0)),
            pl.BlockSpec((1, nk, W_A, tk), lambda i, j: (i, 0, 0, 0)),
            pl.BlockSpec((1, H_I * D_I, tq), lambda i, j: (i, 0, j)),
            pl.BlockSpec((1, H_I, tq), lambda i, j: (i, 0, j)),
            pl.BlockSpec((1, t, D_I), lambda i, j: (i, 0, 0)),
        ],
        out_specs=pl.BlockSpec((1, tq, W_A), lambda i, j: (i, j, 0)),
        out_shape=jax.ShapeDtypeStruct((b, t, W_A), F32),
        scratch_shapes=[
            pltpu.VMEM((nk, tk, tq), F32),
            pltpu.VMEM((CHAINS_PER_STAGE, nk, tk, tq), F32),
            pltpu.VMEM((CHAINS_PER_STAGE, nk, tk, tq), BF16),
            pltpu.VMEM((W_A, tq), F32),
        ],
        compiler_params=_cparams(("parallel", "arbitrary")),
        name="prompt_a",
    )(qat, ka, vta, qit, wt, ki)


def _lambda_of(lq1_ref, lk1_ref, lq2_ref, lk2_ref, lambda_init):
    s1 = jnp.sum(lq1_ref[...] * lk1_ref[...], axis=1, keepdims=True)
    s2 = jnp.sum(lq2_ref[...] * lk2_ref[...], axis=1, keepdims=True)
    return jnp.exp(s1) - jnp.exp(s2) + lambda_init


def _prompt_b_kernel(qt_ref, k_ref, vt_ref, lq1_ref, lk1_ref, lq2_ref, lk2_ref, g_ref, o_ref,
                     s_scr, p_scr, ot_scr, *, tq, tk, lambda_init):
    qi = pl.program_id(1)
    nkb, kl, ql, koff = _block_geometry(qi, tq, tk)
    causal_last = (kl + koff(nkb - 1)) <= ql
    lam = _lambda_of(lq1_ref, lk1_ref, lq2_ref, lk2_ref, lambda_init)
    slopes = _slopes2(H_B)
    row = lax.broadcasted_iota(I32, (LANES, tq), 0)
    pos = _alibi_pos(tk)
    qts = []
    for h in range(H_B):
        qp = qt_ref[0, h * LANES:(h + 1) * LANES, :]
        coef = _alibi_coef(slopes[h], tq)
        qts.append(jnp.concatenate([jnp.where(row < HD_B, qp, jnp.zeros_like(qp)), coef], axis=0))
        qts.append(jnp.concatenate([jnp.where(row >= HD_B, qp, jnp.zeros_like(qp)), coef], axis=0))

    def scores(c, kb, is_last):
        h = c // 2
        kblk = k_ref[0, pl.ds(pl.multiple_of(kb * tk, tk), tk), h * LANES:(h + 1) * LANES]
        s = _dot(jnp.concatenate([kblk, pos], axis=1), qts[c])
        return jnp.where(causal_last, s, NEG) if is_last else s

    def offset(c, kb):
        return slopes[c // 2] * koff(kb).astype(F32)

    def values(c, kb):
        h = c // 2
        return vt_ref[0, kb, h * LANES:(h + 1) * LANES, :]

    ls, accs = _chain_pipeline(2 * H_B, CHAINS_PER_STAGE, nkb, tq, LANES, scores, offset, values,
                               s_scr, p_scr, peel_last=True)
    g = g_ref[...]
    for h in range(H_B):
        o = accs[2 * h] / ls[2 * h] - lam * (accs[2 * h + 1] / ls[2 * h + 1])
        ms = jnp.mean(o * o, axis=0, keepdims=True)
        o = o * lax.rsqrt(ms + RMS_EPS) * g
        ot_scr[h * LANES:(h + 1) * LANES, :] = o * (1.0 - lambda_init)
    o_ref[0] = ot_scr[...].T.astype(o_ref.dtype)


def _prompt_b(qbt, kb, vtb, lq1, lk1, lq2, lk2, g_col, tq, tk, lambda_init):
    b, t, _ = kb.shape
    nq, nk = t // tq, t // tk
    kern = functools.partial(_prompt_b_kernel, tq=tq, tk=tk, lambda_init=lambda_init)
    small = lambda a: pl.BlockSpec(a.shape, lambda i, j: (0, 0))
    return pl.pallas_call(
        kern,
        grid=(b, nq),
        in_specs=[
            pl.BlockSpec((1, W_B, tq), lambda i, j: (i, 0, j)),
            pl.BlockSpec((1, t, W_B), lambda i, j: (i, 0, 0)),
            pl.BlockSpec((1, nk, W_B, tk), lambda i, j: (i, 0, 0, 0)),
            small(lq1), small(lk1), small(lq2), small(lk2), small(g_col),
        ],
        out_specs=pl.BlockSpec((1, tq, W_B), lambda i, j: (i, j, 0)),
        out_shape=jax.ShapeDtypeStruct((b, t, W_B), F32),
        scratch_shapes=[
            pltpu.VMEM((CHAINS_PER_STAGE, nk, tk, tq), F32),
            pltpu.VMEM((CHAINS_PER_STAGE, nk, tk, tq), BF16),
            pltpu.VMEM((W_B, tq), F32),
        ],
        compiler_params=_cparams(("parallel", "arbitrary")),
        name="prompt_b",
    )(qbt, kb, vtb, lq1, lk1, lq2, lk2, g_col)


def _sigmoid(x):
    return 1.0 / (1.0 + jnp.exp(-x))


def _merge_kernel(x_ref, oa_ref, ob_ref, wz_ref, bz_ref, wpa_ref, wpb_ref, wout_ref,
                  lng_ref, lnb_ref, y_ref, *, alpha):
    x = x_ref[...]
    xh = x.astype(BF16)

    def proj(c0, n):
        return _dot_nt(xh, wz_ref[c0:c0 + n, :]) + bz_ref[:, c0:c0 + n]

    d = x.shape[1]
    za = proj(0, W_A)
    ua = (oa_ref[...] * (za * _sigmoid(za))).astype(BF16)
    ya = _dot(ua, wpa_ref[...])
    zb = proj(W_A, W_B)
    ub = (ob_ref[...] * (zb * _sigmoid(zb))).astype(BF16)
    yb = _dot(ub, wpb_ref[...])
    m = _sigmoid(proj(W_A + W_B, d)) * ya + _sigmoid(proj(W_A + W_B + d, d)) * yb
    r = alpha * x + _dot(m.astype(BF16), wout_ref[...])
    mu = jnp.mean(r, axis=1, keepdims=True)
    rc = r - mu
    var = jnp.mean(rc * rc, axis=1, keepdims=True)
    y_ref[...] = rc * lax.rsqrt(var + LN_EPS) * lng_ref[...] + lnb_ref[...]


def _merge(x2d, oa, ob, wz, bz, wpa, wpb, wout, lng, lnb, tm, alpha):
    n, d = x2d.shape
    row = lambda w: pl.BlockSpec((tm, w), lambda i: (i, 0))
    full = lambda a: pl.BlockSpec(a.shape, lambda i: (0, 0))
    return pl.pallas_call(
        functools.partial(_merge_kernel, alpha=alpha),
        grid=(n // tm,),
        in_specs=[row(d), row(W_A), row(W_B), full(wz), full(bz), full(wpa), full(wpb),
                  full(wout), full(lng), full(lnb)],
        out_specs=row(d),
        out_shape=jax.ShapeDtypeStruct((n, d), F32),
        compiler_params=_cparams(("parallel",)),
        name="merge",
    )(x2d, oa, ob, wz, bz, wpa, wpb, wout, lng, lnb)


def _page_specs(n, shape, first_page):
    def spec(j):
        return pl.BlockSpec((1,) + shape, lambda i, s, pt, j=j: (pt[i, first_page(s) + j], 0, 0))
    return [spec(j) for j in range(n)]


def _sample_idx_kernel(pt_ref, qi_ref, wcol_ref, kin_ref, *rest, n_pages, pps, n_tok):
    kip_refs, sc_ref = rest[:pps], rest[pps]
    step = pl.program_id(1)
    n_steps = n_pages // pps
    lane = lax.broadcasted_iota(I32, (SUBLANES, PAGE), 1)
    trow = lax.broadcasted_iota(I32, (SUBLANES, PAGE), 0)

    def scores(kpages_t):
        r = _dot(qi_ref[0], kpages_t.astype(BF16))
        r = jnp.maximum(r, 0.0) * wcol_ref[0]
        per_tok = [jnp.sum(r[t * H_I:(t + 1) * H_I, :], axis=0, keepdims=True)
                   for t in range(n_tok)]
        pad = [jnp.zeros_like(per_tok[0])] * (SUBLANES - n_tok)
        return jnp.concatenate(per_tok + pad, axis=0)

    s_all = scores(jnp.concatenate([r[0] for r in kip_refs], axis=1))
    for j in range(pps):
        sc_ref[0, step * pps + j] = s_all[:, j * PAGE:(j + 1) * PAGE]

    @pl.when(step == n_steps - 1)
    def _():
        s = jnp.where(lane <= trow, scores(kin_ref[0]), NEG)
        sc_ref[0, n_pages] = jnp.where(lane < n_tok, s, PAD_SCORE)


def _sample_select_kernel(sc_ref, bias_ref, *, n_tok, topk):
    nb, n_tiles = sc_ref.shape[0], sc_ref.shape[1]
    rows = nb * SUBLANES
    lane = lax.broadcasted_iota(I32, (rows, PAGE), 1)
    trow = lax.broadcasted_iota(I32, (rows, PAGE), 0) % SUBLANES
    k = float(topk)
    tile = lambda j: sc_ref[:, j].reshape(rows, PAGE)

    def count_ge(v):
        cnt = jnp.zeros((rows, PAGE), F32)
        for j in range(n_tiles):
            cnt = jnp.where(tile(j) >= v, cnt + 1.0, cnt)
        return jnp.sum(cnt, axis=1, keepdims=True)

    lo, hi = _kth_largest_cell(count_ge, (rows, 1), k)
    need = k - count_ge(hi)
    before = (lax.broadcasted_iota(I32, (PAGE, PAGE), 0)
              < lax.broadcasted_iota(I32, (PAGE, PAGE), 1)).astype(BF16)
    seen = jnp.zeros((rows, 1), F32)
    for j in range(n_tiles):
        t = tile(j)
        eq = (t >= lo) & (t < hi)
        eqf = jnp.where(eq, 1.0, 0.0)
        rank = _dot(eqf.astype(BF16), before) + seen
        sel = (t >= hi) | (eq & (rank < need))
        if j == n_tiles - 1:
            sel = sel & (lane <= trow) & (lane < n_tok)
        bias_ref[:, j] = jnp.where(sel, 0.0, NEG).reshape(nb, SUBLANES, PAGE)
        seen = seen + jnp.sum(eqf, axis=1, keepdims=True)


def _sample_select(scores, n_tok, topk):
    return pl.pallas_call(
        functools.partial(_sample_select_kernel, n_tok=n_tok, topk=topk),
        out_shape=jax.ShapeDtypeStruct(scores.shape, F32),
        compiler_params=pltpu.CompilerParams(vmem_limit_bytes=VMEM_LIMIT),
        name="sample_select",
    )(scores)


def _sample_idx(page_table, qi_rows, wcol, cache_ki_t, ki_new_t, n_tok):
    b, n_pages = page_table.shape
    rows = n_tok * H_I
    pps = min(IDX_PAGES_PER_STEP, n_pages)
    kern = functools.partial(_sample_idx_kernel, n_pages=n_pages, pps=pps, n_tok=n_tok)
    grid_spec = pltpu.PrefetchScalarGridSpec(
        num_scalar_prefetch=1,
        grid=(b, n_pages // pps),
        in_specs=[
            pl.BlockSpec((1, rows, D_I), lambda i, s, pt: (i, 0, 0)),
            pl.BlockSpec((1, rows, 1), lambda i, s, pt: (i, 0, 0)),
            pl.BlockSpec((1, D_I, PAGE), lambda i, s, pt: (i, 0, 0)),
        ] + _page_specs(pps, (D_I, PAGE), lambda s: s * pps),
        out_specs=pl.BlockSpec((1, n_pages + 1, SUBLANES, PAGE), lambda i, s, pt: (i, 0, 0, 0)),
    )
    return pl.pallas_call(
        kern,
        grid_spec=grid_spec,
        out_shape=jax.ShapeDtypeStruct((b, n_pages + 1, SUBLANES, PAGE), F32),
        compiler_params=_cparams(("parallel", "arbitrary")),
        name="sample_idx",
    )(page_table, qi_rows, wcol, ki_new_t, *([cache_ki_t] * pps))


def _softmax_update(s, m_scr, l_scr):
    m = m_scr[...]
    m_new = jnp.maximum(m, jnp.max(s, axis=1, keepdims=True))
    p = jnp.exp2(s - m_new)
    alpha = jnp.exp2(m - m_new)
    l_scr[...] = alpha * l_scr[...] + jnp.sum(p, axis=1, keepdims=True)
    m_scr[...] = m_new
    return p, alpha


def _decode_a_kernel(pt_ref, wq_ref, slope_ref, kn_ref, vn_ref, bias_ref, biasn_ref, *rest,
                     n_pages, pps, n_tok, past_len):
    kp_refs, vp_refs = rest[:pps], rest[pps:2 * pps]
    o_ref, acc_scr, m_scr, l_scr = rest[2 * pps:]
    step = pl.program_id(1)
    n_steps = n_pages // pps
    rows = n_tok * H_A

    @pl.when(step == 0)
    def _():
        acc_scr[...] = jnp.zeros_like(acc_scr)
        m_scr[...] = jnp.full_like(m_scr, NEG)
        l_scr[...] = jnp.zeros_like(l_scr)

    def update(k_ts, v_ts, first_pos, b8s, new_keys):
        n = len(k_ts)
        lane = lax.broadcasted_iota(I32, (rows, n * PAGE), 1)
        trow = lax.broadcasted_iota(I32, (rows, n * PAGE), 0) // H_A
        s = jnp.concatenate([_dot(wq_ref[0], k_t.astype(BF16)) for k_t in k_ts], axis=1)
        s = s - slope_ref[...] * (past_len + trow - (first_pos + lane)).astype(F32)
        s = s + jnp.concatenate(
            [jnp.broadcast_to(jnp.concatenate([b8[t:t + 1, :] for b8 in b8s], axis=1),
                              (H_A, n * PAGE)) for t in range(n_tok)], axis=0)
        if new_keys:
            s = jnp.where((lane <= trow) & (lane < n_tok), s, NEG)
        p, alpha = _softmax_update(s, m_scr, l_scr)
        p = p.astype(BF16)
        pv = _dot_nt(p[:, :PAGE], v_ts[0].astype(BF16))
        for j in range(1, n):
            pv = pv + _dot_nt(p[:, j * PAGE:(j + 1) * PAGE], v_ts[j].astype(BF16))
        acc_scr[...] = acc_scr[...] * alpha + pv

    update([r[0] for r in kp_refs], [r[0] for r in vp_refs], step * (pps * PAGE),
           [bias_ref[0, j] for j in range(pps)], False)

    @pl.when(step == n_steps - 1)
    def _():
        update([kn_ref[0]], [vn_ref[0]], past_len, [biasn_ref[0, 0]], True)
        o = acc_scr[...] / l_scr[...]
        own = (lax.broadcasted_iota(I32, (rows, W_A), 1) // HD_A
               == lax.broadcasted_iota(I32, (rows, W_A), 0) % H_A)
        o = jnp.where(own, o, 0.0)
        o_ref[0] = jnp.concatenate(
            [jnp.sum(o[t * H_A:(t + 1) * H_A, :], axis=0, keepdims=True) for t in range(n_tok)],
            axis=0)


def _decode_a(page_table, wq, slope_col, cache_k_t, cache_v_t, k_new_t, v_new_t, bias, n_tok,
              past_len):
    b, n_pages = page_table.shape
    rows = n_tok * H_A
    pps = min(PAGES_PER_STEP, n_pages)
    kern = functools.partial(_decode_a_kernel, n_pages=n_pages, pps=pps, n_tok=n_tok,
                             past_len=past_len)
    pages = _page_specs(pps, (W_A, PAGE), lambda s: s * pps)
    per_batch = lambda shape: pl.BlockSpec((1,) + shape, lambda i, s, pt: (i,) + (0,) * len(shape))
    grid_spec = pltpu.PrefetchScalarGridSpec(
        num_scalar_prefetch=1,
        grid=(b, n_pages // pps),
        in_specs=[
            per_batch((rows, W_A)),
            pl.BlockSpec((rows, 1), lambda i, s, pt: (0, 0)),
            per_batch((W_A, PAGE)), per_batch((W_A, PAGE)),
            pl.BlockSpec((1, pps, SUBLANES, PAGE), lambda i, s, pt: (i, s, 0, 0)),
            pl.BlockSpec((1, 1, SUBLANES, PAGE), lambda i, s, pt: (i, n_pages, 0, 0)),
        ] + pages + pages,
        out_specs=per_batch((n_tok, W_A)),
        scratch_shapes=[pltpu.VMEM((rows, W_A), F32), pltpu.VMEM((rows, 1), F32),
                        pltpu.VMEM((rows, 1), F32)],
    )
    return pl.pallas_call(
        kern,
        grid_spec=grid_spec,
        out_shape=jax.ShapeDtypeStruct((b, n_tok, W_A), F32),
        compiler_params=_cparams(("parallel", "arbitrary")),
        name="decode_a",
    )(page_table, wq, slope_col, k_new_t, v_new_t, bias, bias,
      *([cache_k_t] * pps), *([cache_v_t] * pps))


def _decode_b_kernel(pt_ref, wq_ref, kn_ref, vn_ref, lq1_ref, lk1_ref, lq2_ref, lk2_ref, g_ref,
                     *rest, n_pages, pps, n_tok, past_len, lambda_init):
    kp_refs, vp_refs = rest[:pps], rest[pps:2 * pps]
    o_ref, acc_scr, m_scr, l_scr = rest[2 * pps:]
    step = pl.program_id(1)
    n_steps = n_pages // pps
    hrows = 2 * n_tok
    rows = H_B * hrows
    slope_col = jnp.concatenate([jnp.full((hrows, 1), s, F32) for s in _slopes2(H_B)], axis=0)

    @pl.when(step == 0)
    def _():
        acc_scr[...] = jnp.zeros_like(acc_scr)
        m_scr[...] = jnp.full_like(m_scr, NEG)
        l_scr[...] = jnp.zeros_like(l_scr)

    def update(k_hs, v_hs, first_pos, new_keys):
        n = len(k_hs[0])
        lane = lax.broadcasted_iota(I32, (rows, n * PAGE), 1)
        trow = lax.broadcasted_iota(I32, (rows, n * PAGE), 0) % n_tok
        s = jnp.concatenate(
            [jnp.concatenate([_dot_nt(wq_ref[0, h], k.astype(BF16)) for k in k_hs[h]], axis=1)
             for h in range(H_B)], axis=0)
        s = s - slope_col * (past_len + trow - (first_pos + lane)).astype(F32)
        if new_keys:
            s = jnp.where((lane <= trow) & (lane < n_tok), s, NEG)
        p, alpha = _softmax_update(s, m_scr, l_scr)
        p = p.astype(BF16)
        pvs = []
        for h in range(H_B):
            ph = p[h * hrows:(h + 1) * hrows, :]
            pv = _dot(ph[:, :PAGE], v_hs[h][0].astype(BF16))
            for j in range(1, n):
                pv = pv + _dot(ph[:, j * PAGE:(j + 1) * PAGE], v_hs[h][j].astype(BF16))
            pvs.append(pv)
        acc_scr[...] = acc_scr[...] * alpha + jnp.concatenate(pvs, axis=0)

    head_rows = lambda refs, h: [r[0, pl.ds(h, PAGE, stride=H_B), :] for r in refs]
    update([head_rows(kp_refs, h) for h in range(H_B)],
           [head_rows(vp_refs, h) for h in range(H_B)], step * (pps * PAGE), False)

    @pl.when(step == n_steps - 1)
    def _():
        update([[kn_ref[0, h]] for h in range(H_B)], [[vn_ref[0, h]] for h in range(H_B)],
               past_len, True)
        lam = _lambda_of(lq1_ref, lk1_ref, lq2_ref, lk2_ref, lambda_init)
        o_all = acc_scr[...] / l_scr[...]
        for h in range(H_B):
            o = (o_all[h * hrows:h * hrows + n_tok, :]
                 - lam * o_all[h * hrows + n_tok:(h + 1) * hrows, :])
            ms = jnp.mean(o * o, axis=1, keepdims=True)
            o = o * lax.rsqrt(ms + RMS_EPS) * g_ref[...] * (1.0 - lambda_init)
            o_ref[0, :, h * LANES:(h + 1) * LANES] = o


def _decode_b(page_table, wq, cache_k_r, cache_v_r, k_new, v_new, lam_args, g_row, n_tok,
              past_len, lambda_init):
    b, n_pages = page_table.shape
    rows = 2 * n_tok
    pps = min(PAGES_PER_STEP, n_pages)
    kern = functools.partial(_decode_b_kernel, n_pages=n_pages, pps=pps, n_tok=n_tok,
                             past_len=past_len, lambda_init=lambda_init)
    pages = _page_specs(pps, (PAGE * H_B, 2 * HD_B), lambda s: s * pps)
    per_batch = lambda shape: pl.BlockSpec((1,) + shape, lambda i, s, pt: (i,) + (0,) * len(shape))
    small = lambda a: pl.BlockSpec(a.shape, lambda i, s, pt: (0, 0))
    grid_spec = pltpu.PrefetchScalarGridSpec(
        num_scalar_prefetch=1,
        grid=(b, n_pages // pps),
        in_specs=[
            per_batch((H_B, rows, 2 * HD_B)),
            per_batch((H_B, PAGE, 2 * HD_B)), per_batch((H_B, PAGE, 2 * HD_B)),
        ] + [small(a) for a in lam_args] + [small(g_row)] + pages + pages,
        out_specs=per_batch((n_tok, W_B)),
        scratch_shapes=[pltpu.VMEM((H_B * rows, 2 * HD_B), F32), pltpu.VMEM((H_B * rows, 1), F32),
                        pltpu.VMEM((H_B * rows, 1), F32)],
    )
    return pl.pallas_call(
        kern,
        grid_spec=grid_spec,
        out_shape=jax.ShapeDtypeStruct((b, n_tok, W_B), F32),
        compiler_params=_cparams(("parallel", "arbitrary")),
        name="decode_b",
    )(page_table, wq, k_new, v_new, *lam_args, g_row,
      *([cache_k_r] * pps), *([cache_v_r] * pps))


def _split_rows(wt, d_model):
    out, off = {}, 0
    for name, n in SPLITS:
        n = d_model if n is None else n
        out[name] = wt[off:off + n]
        off += n
    return out


def _pad_rows(a, rows):
    return jnp.pad(a, [(0, rows - a.shape[0])] + [(0, 0)] * (a.ndim - 1))


def _proj_weights(w_in, b_in, d_model):
    w = _split_rows(w_in.T, d_model)
    b = _split_rows(b_in, d_model)
    order1 = ['q_a', 'k_a', 'v_a', 'q_b', 'k_b', 'v_b', 'q_idx']
    group1 = lambda p: jnp.concatenate(
        [p[n] for n in order1] + [_pad_rows(p['k_idx'], LANES), _pad_rows(p['w_idx'], LANES)])
    order2 = ['z_a', 'z_b', 'g_a', 'g_b']
    group2 = lambda p: jnp.concatenate([p[n] for n in order2])
    return (group1(w).astype(BF16), group1(b)[None, :], group2(w).astype(BF16),
            group2(b)[None, :])


def _prompt_branches(xp, bp, t, w1t, b1, lam_args, g, tq, tk, topk, lambda_init):
    (qat, kat32, ka16, vat32, vat, qbt, kb32, kb16, vb32, vbt, qit, kit32, ki16, wit) = _project(
        xp, w1t, b1, tk, batch=bp)
    r3 = lambda a: a.reshape(bp, t, a.shape[-1])
    o_a = _prompt_a(qat, r3(ka16), vat, qit, wit, r3(ki16), tq, tk, topk)
    o_b = _prompt_b(qbt, r3(kb16), vbt, *lam_args, g[:, None], tq, tk, lambda_init)
    kv = (kat32.reshape(bp, H_A, HD_A, t).transpose(0, 3, 1, 2),
          vat32.reshape(bp, H_A, HD_A, t).transpose(0, 3, 1, 2),
          kit32.transpose(0, 2, 1),
          kb32.reshape(bp, t, H_B, 2 * HD_B), vb32.reshape(bp, t, H_B, 2 * HD_B))
    return o_a.reshape(bp * t, W_A), o_b.reshape(bp * t, W_B), kv


def _sample_branches(xs, bs, ts, page_table, c_k_a, c_v_a, c_k_idx, c_k_b, c_v_b, w1t, b1,
                     lam_args, g, topk, lambda_init):
    n_pool = c_k_a.shape[0]
    past_len = page_table.shape[1] * PAGE
    qa, ka32, va32, qb, kb32, vb32, qidx, ki32, wi = _project(xs, w1t, b1, bs * ts)
    s3 = lambda a: a.reshape(bs, ts, a.shape[-1])
    new_t = lambda a: jnp.pad(s3(a).transpose(0, 2, 1), ((0, 0), (0, 0), (0, PAGE - ts)))
    qi_rows = s3(qidx).reshape(bs, ts * H_I, D_I)
    wcol = s3(wi).reshape(bs, ts * H_I, 1)
    scores = _sample_idx(page_table, qi_rows, wcol, c_k_idx.transpose(0, 2, 1), new_t(ki32), ts)
    bias = _sample_select(scores, ts, topk)
    eye_a = jnp.eye(H_A, dtype=BF16)
    wq_a = (s3(qa).reshape(bs, ts, H_A, 1, HD_A) * eye_a[None, None, :, :, None]
            ).reshape(bs, ts * H_A, W_A)
    slope_a = jnp.tile(jnp.asarray(_slopes2(H_A), F32), ts)[:, None]
    page_t = lambda c: c.transpose(0, 2, 3, 1).reshape(n_pool, W_A, PAGE)
    o_a = _decode_a(page_table, wq_a, slope_a, page_t(c_k_a), page_t(c_v_a), new_t(ka32),
                    new_t(va32), bias, ts, past_len)
    qb4 = s3(qb).reshape(bs, ts, H_B, 2 * HD_B).transpose(0, 2, 1, 3)
    first = (jnp.arange(2 * HD_B) < HD_B)
    wq_b = jnp.concatenate([jnp.where(first, qb4, 0), jnp.where(first, 0, qb4)], axis=2)
    new_h = lambda a: jnp.pad(a.reshape(bs, ts, H_B, 2 * HD_B).transpose(0, 2, 1, 3),
                              ((0, 0), (0, 0), (0, PAGE - ts), (0, 0)))
    page_r = lambda c: c.reshape(n_pool, PAGE * H_B, 2 * HD_B)
    o_b = _decode_b(page_table, wq_b, page_r(c_k_b), page_r(c_v_b), new_h(kb32), new_h(vb32),
                    lam_args, g[None, :], ts, past_len, lambda_init)
    return (o_a.reshape(bs * ts, W_A), o_b.reshape(bs * ts, W_B),
            (ka32, va32, ki32, kb32, vb32))


def kernel(x_prompt, x_sample, cache_k_a, cache_v_a, cache_k_idx, cache_k_b, cache_v_b, page_table,
           w_in, b_in, lambda_q1, lambda_k1, lambda_q2, lambda_k2, subln_g, w_pa, w_pb, w_out,
           ln_g, ln_b):
    depth = w_in.shape[0]
    bp, t, d = x_prompt.shape
    bs, ts, _ = x_sample.shape
    past_len = page_table.shape[1] * PAGE
    alpha = (2.0 * depth) ** 0.25
    tq = min(256, t)
    tk = min(512, t)
    tm = min(512, bp * t)
    topk_p = min(TOPK_MAX, t // 4)
    topk_s = min(TOPK_MAX, (past_len + ts) // 4)

    xp = x_prompt.reshape(bp * t, d)
    xs = x_sample.reshape(bs * ts, d)
    pk, sk = [], []
    for l in range(depth):
        lambda_init = 0.8 - 0.6 * math.exp(-0.3 * l)
        w1t, b1, wzt, bz = _proj_weights(w_in[l], b_in[l], d)
        lam_args = (lambda_q1[l][None, :], lambda_k1[l][None, :],
                    lambda_q2[l][None, :], lambda_k2[l][None, :])
        merge_w = (wzt, bz, w_pa[l].astype(BF16), w_pb[l].astype(BF16), w_out[l].astype(BF16),
                   ln_g[l][None, :], ln_b[l][None, :])
        o_a, o_b, kv = _prompt_branches(xp, bp, t, w1t, b1, lam_args, subln_g[l], tq, tk, topk_p,
                                        lambda_init)
        pk.append(kv)
        xp = _merge(xp, o_a, o_b, *merge_w, tm, alpha)
        o_a, o_b, kv = _sample_branches(xs, bs, ts, page_table, cache_k_a[l], cache_v_a[l],
                                        cache_k_idx[l], cache_k_b[l], cache_v_b[l], w1t, b1,
                                        lam_args, subln_g[l], topk_s, lambda_init)
        sk.append(kv)
        xs = _merge(xs, o_a, o_b, *merge_w, bs * ts, alpha)

    def stacked(layers, b, n):
        shapes = ((b, n, H_A, HD_A), (b, n, H_A, HD_A), (b, n, D_I),
                  (b, n, H_B, 2 * HD_B), (b, n, H_B, 2 * HD_B))
        return tuple(jnp.stack([kv[i].reshape(s) for kv in layers]) for i, s in enumerate(shapes))

    return (xp.reshape(bp, t, d), xs.reshape(bs, ts, d)) + stacked(pk, bp, t) + stacked(sk, bs, ts)
```

```python
import functools
import math

import jax
import jax.numpy as jnp
import numpy as np
from jax import lax
from jax.experimental import pallas as pl
from jax.experimental.pallas import tpu as pltpu

F32 = jnp.float32
BF16 = jnp.bfloat16
I32 = jnp.int32

H_A, HD_A = 8, 64
H_I, D_I = 8, 32
H_B, HD_B = 4, 64
TOPK_MAX = 256
PAGE = 128
LN_EPS = 1e-5
RMS_EPS = 1e-5
NEG = -1e30
PAD_SCORE = -3e38
LOG2E = math.log2(math.e)

W_A = H_A * HD_A
W_B = H_B * 2 * HD_B
LANES = 128
SUBLANES = 8
VMEM_LIMIT = 56 * 1024 * 1024
TIE_REFINE_STEPS = 16
PAGES_PER_STEP = 32
IDX_PAGES_PER_STEP = 64
CHAINS_PER_STAGE = 2
PART_ROWS = 64

SPLITS = (
    ('q_a', W_A), ('k_a', W_A), ('v_a', W_A),
    ('q_idx', H_I * D_I), ('k_idx', D_I), ('w_idx', H_I), ('z_a', W_A),
    ('q_b', W_B), ('k_b', W_B), ('v_b', W_B), ('z_b', W_B),
    ('g_a', None), ('g_b', None),
)


def _slopes2(n):
    return tuple(LOG2E * 2.0 ** (-8.0 * (i + 1) / n) for i in range(n))


def _cparams(sem):
    return pltpu.CompilerParams(dimension_semantics=sem, vmem_limit_bytes=VMEM_LIMIT)


def _dot_nt(a, b):
    return lax.dot_general(a, b, (((1,), (1,)), ((), ())), preferred_element_type=F32)


def _dot(a, b):
    return jnp.dot(a, b, preferred_element_type=F32)


_C_QA, _C_KA, _C_VA, _C_QB, _C_KB, _C_VB = 0, 512, 1024, 1536, 2048, 2560
_C_QI, _C_KI, _C_WI, _C_END = 3072, 3328, 3456, 3584


def _proj_kernel(x_ref, w_ref, b_ref, *out_refs, key_major):
    x = x_ref[...].astype(BF16)

    def proj(c0, n):
        return _dot_nt(x, w_ref[c0:c0 + n, :]) + b_ref[:, c0:c0 + n]

    qa = proj(_C_QA, W_A) * (LOG2E * HD_A ** -0.5)
    ka, va = proj(_C_KA, W_A), proj(_C_VA, W_A)
    qb = proj(_C_QB, W_B) * (LOG2E * HD_B ** -0.5)
    kb, vb = proj(_C_KB, W_B), proj(_C_VB, W_B)
    qi = proj(_C_QI, H_I * D_I)
    ki = proj(_C_KI, LANES)
    wi = proj(_C_WI, LANES) * ((H_I ** -0.5) * (D_I ** -0.5))
    if key_major:
        (qat_ref, kat32_ref, ka16_ref, vat32_ref, vat_ref, qbt_ref, kb32_ref, kb16_ref, vb32_ref,
         vbt_ref, qit_ref, kit32_ref, ki16_ref, wit_ref) = out_refs
        qat_ref[0] = qa.T.astype(BF16)
        qbt_ref[0] = qb.T.astype(BF16)
        kat32_ref[0] = ka.T
        ka16_ref[...] = ka.astype(BF16)
        kb16_ref[...] = kb.astype(BF16)
        va_t, vb_t = va.T, vb.T
        vat32_ref[0] = va_t
        vat_ref[0, 0] = va_t.astype(BF16)
        vbt_ref[0, 0] = vb_t.astype(BF16)
        qit_ref[0] = qi.T.astype(BF16)
        kit32_ref[0] = ki.T[:D_I, :]
        ki16_ref[...] = ki[:, :D_I].astype(BF16)
        wit_ref[0] = wi.T[:H_I, :]
    else:
        qa_ref, ka32_ref, va32_ref, qb_ref, kb32_ref, vb32_ref, qi_ref, ki32_ref, wi_ref = out_refs
        qa_ref[...] = qa.astype(BF16)
        qb_ref[...] = qb.astype(BF16)
        ka32_ref[...] = ka
        va32_ref[...] = va
        qi_ref[...] = qi.astype(BF16)
        ki32_ref[...] = ki[:, :D_I]
        wi_ref[...] = wi[:, :H_I]
    hw = 2 * HD_B
    for h in range(H_B):
        rows_h = pl.ds(h, x.shape[0], stride=H_B)
        kb32_ref[rows_h, :] = kb[:, h * hw:(h + 1) * hw]
        vb32_ref[rows_h, :] = vb[:, h * hw:(h + 1) * hw]


def _project(x2d, w1t, b1, tm, batch=None):
    n, d = x2d.shape
    sds = jax.ShapeDtypeStruct
    row = lambda w, dt: (pl.BlockSpec((tm, w), lambda i: (i, 0)), sds((n, w), dt))
    full = lambda a: pl.BlockSpec(a.shape, lambda i: (0, 0))
    head_rows = (pl.BlockSpec((tm * H_B, 2 * HD_B), lambda i: (i, 0)),
                 sds((n * H_B, 2 * HD_B), F32))
    if batch is None:
        outs = [row(W_A, BF16), row(W_A, F32), row(W_A, F32), row(W_B, BF16), head_rows,
                head_rows, row(H_I * D_I, BF16), row(D_I, F32), row(H_I, F32)]
    else:
        t = n // batch
        nk = t // tm
        blocks = lambda w: (pl.BlockSpec((1, 1, w, tm), lambda i: (i // nk, i % nk, 0, 0)),
                            sds((batch, nk, w, tm), BF16))
        feat = lambda f, dt: (pl.BlockSpec((1, f, tm), lambda i: (i // nk, 0, i % nk)),
                              sds((batch, f, t), dt))
        outs = [feat(W_A, BF16), feat(W_A, F32), row(W_A, BF16), feat(W_A, F32), blocks(W_A),
                feat(W_B, BF16), head_rows, row(W_B, BF16), head_rows, blocks(W_B),
                feat(H_I * D_I, BF16), feat(D_I, F32), row(D_I, BF16), feat(H_I, F32)]
    return pl.pallas_call(
        functools.partial(_proj_kernel, key_major=batch is not None),
        grid=(n // tm,),
        in_specs=[pl.BlockSpec((tm, d), lambda i: (i, 0)), full(w1t), full(b1)],
        out_specs=[o[0] for o in outs],
        out_shape=[o[1] for o in outs],
        compiler_params=_cparams(("parallel",)),
        name="proj",
    )(x2d, w1t, b1)


def _float_of_key(u):
    s = u ^ jnp.int32(-2 ** 31)
    fb = jnp.where(s >= 0, s, s ^ jnp.int32(0x7FFFFFFF))
    return lax.bitcast_convert_type(fb, F32)


def _kth_largest_cell(count_ge, shape, k):
    def bit_body(i, prefix):
        bit = lax.shift_left(jnp.int32(1), jnp.int32(31) - i)
        cand = prefix | bit
        return jnp.where(count_ge(_float_of_key(cand)) >= k, cand, prefix)

    prefix = lax.fori_loop(0, 32, bit_body, jnp.zeros(shape, I32))

    def refine(_, c):
        lo, hi = c
        mid = lo + (hi - lo) * 0.5
        ok = count_ge(mid) >= k
        return jnp.where(ok, mid, lo), jnp.where(ok, hi, mid)

    return lax.fori_loop(0, TIE_REFINE_STEPS, refine,
                         (_float_of_key(prefix), _float_of_key(prefix + 1)))


def _fold_rows(x, op):
    out = x[:PART_ROWS]
    for i in range(1, x.shape[0] // PART_ROWS):
        out = op(out, x[i * PART_ROWS:(i + 1) * PART_ROWS])
    return out


def _split3(x):
    out = []
    for _ in range(3):
        part = float(np.asarray(x, np.float32).astype(BF16))
        out.append(part)
        x -= part
    return out


def _alibi_pos(tk):
    kl = lax.broadcasted_iota(I32, (tk, LANES), 0)
    col = lax.broadcasted_iota(I32, (tk, LANES), 1)
    hi = lax.shift_right_logical(kl, 6)
    lo = kl & 63
    return jnp.where(col < 3, hi, jnp.where(col < 6, lo, 0)).astype(F32).astype(BF16)


def _alibi_coef(slope, tq):
    row = lax.broadcasted_iota(I32, (LANES, tq), 0)
    coef = jnp.zeros((LANES, tq), F32)
    for i, v in enumerate(_split3(64.0 * slope) + _split3(slope)):
        coef = jnp.where(row == i, v, coef)
    return coef.astype(BF16)


def _chain_pipeline(n_chain, g, nkb, tq, acc_rows, scores, offset, values, s_scr, p_scr,
                    peel_last=False):
    n_group = n_chain // g
    neg0 = jnp.full((PART_ROWS, tq), NEG, F32)
    zero = jnp.zeros((PART_ROWS, tq), F32)
    zacc = jnp.zeros((acc_rows, tq), F32)
    ms, ls, accs = [None] * n_chain, [None] * n_chain, [None] * n_chain
    for grp in range(n_group + 2):
        def body(kb, carry, grp=grp, is_last=False):
            m_a, l_b, acc_c = list(carry[0]), list(carry[1]), list(carry[2])
            for i in range(g):
                if grp >= 2:
                    c = (grp - 2) * g + i
                    acc_c[i] = acc_c[i] + _dot(values(c, kb), p_scr[i, kb])
                if 1 <= grp <= n_group:
                    c = (grp - 1) * g + i
                    p = jnp.exp2(s_scr[i, kb] - (ms[c] - offset(c, kb)))
                    p_scr[i, kb] = p.astype(BF16)
                    l_b[i] = l_b[i] + _fold_rows(p, jnp.add)
                if grp < n_group:
                    c = grp * g + i
                    s = scores(c, kb, is_last)
                    s_scr[i, kb] = s
                    m_a[i] = jnp.maximum(m_a[i], _fold_rows(s, jnp.maximum) + offset(c, kb))
            return tuple(m_a), tuple(l_b), tuple(acc_c)

        n_loop = nkb - 1 if peel_last else nkb
        carry = lax.fori_loop(0, n_loop // 2,
                              lambda j, c, body=body: body(2 * j + 1, body(2 * j, c)),
                              ((neg0,) * g, (zero,) * g, (zacc,) * g))
        carry = lax.cond(n_loop % 2 == 1, lambda c, body=body: body(n_loop - 1, c),
                         lambda c: c, carry)
        m_a, l_b, acc_c = body(nkb - 1, carry, is_last=True) if peel_last else carry
        for i in range(g):
            if grp < n_group:
                ms[grp * g + i] = jnp.max(m_a[i], axis=0, keepdims=True)
            if 1 <= grp <= n_group:
                ls[(grp - 1) * g + i] = jnp.sum(l_b[i], axis=0, keepdims=True)
            if grp >= 2:
                accs[(grp - 2) * g + i] = acc_c[i]
    return ls, accs


def _block_geometry(qi, tq, tk):
    nkb = ((qi + 1) * tq + tk - 1) // tk
    kl = lax.broadcasted_iota(I32, (tk, tq), 0)
    ql = lax.broadcasted_iota(I32, (tk, tq), 1)

    def koff(kb):
        return kb * tk - qi * tq + jnp.zeros((1, tq), I32)

    return nkb, kl, ql, koff


def _prompt_a_kernel(qt_ref, k_ref, vt_ref, qit_ref, wt_ref, ki_ref, o_ref,
                     sc_scr, s_scr, p_scr, ot_scr, *, tq, tk, topk):
    qi = pl.program_id(1)
    nkb, kl, ql, koff = _block_geometry(qi, tq, tk)
    last = nkb - 1
    causal_last = (kl + koff(last)) <= ql
    wt = wt_ref[0]
    k = float(topk)

    def idx_body(kb, c):
        ki = ki_ref[0, pl.ds(pl.multiple_of(kb * tk, tk), tk), :]
        acc = jnp.zeros((tk, tq), F32)
        for h in range(H_I):
            r = _dot(ki, qit_ref[0, h * D_I:(h + 1) * D_I, :])
            acc = acc + jnp.maximum(r, 0.0) * wt[h:h + 1, :]
        sc_scr[kb] = acc
        return c

    lax.fori_loop(0, nkb, idx_body, 0)
    sc_scr[last] = jnp.where(causal_last, sc_scr[last], NEG)

    def count_ge(v):
        def body(kb, cnt):
            t = sc_scr[kb]
            for i in range(tk // PART_ROWS):
                cnt = jnp.where(t[i * PART_ROWS:(i + 1) * PART_ROWS] >= v, cnt + 1.0, cnt)
            return cnt
        part = lax.fori_loop(0, nkb, body, jnp.zeros((PART_ROWS, tq), F32))
        return jnp.sum(part, axis=0, keepdims=True)

    def bit_body(i, prefix):
        bit = lax.shift_left(jnp.int32(1), jnp.int32(31) - i)
        cand = prefix | bit
        return jnp.where(count_ge(_float_of_key(cand)) >= k, cand, prefix)

    prefix = lax.fori_loop(0, 32, bit_body, jnp.zeros((1, tq), I32))
    thr = _float_of_key(prefix)
    tie = jnp.max(count_ge(thr)) > k

    @pl.when(jnp.logical_not(tie))
    def _():
        def sel_body(kb, c):
            sc_scr[kb] = jnp.where(sc_scr[kb] >= thr, 0.0, NEG)
            return c
        lax.fori_loop(0, nkb, sel_body, 0)

    @pl.when(tie)
    def _():
        def refine(_, c):
            lo, hi = c
            mid = lo + (hi - lo) * 0.5
            ok = count_ge(mid) >= k
            return jnp.where(ok, mid, lo), jnp.where(ok, hi, mid)

        lo, hi = lax.fori_loop(0, TIE_REFINE_STEPS, refine, (thr, _float_of_key(prefix + 1)))
        need = k - count_ge(hi)
        below = (lax.broadcasted_iota(I32, (tk, tk), 1)
                 < lax.broadcasted_iota(I32, (tk, tk), 0)).astype(BF16)

        def sel_body(kb, seen):
            t = sc_scr[kb]
            eq = (t >= lo) & (t < hi)
            eqf = jnp.where(eq, 1.0, 0.0)
            rank = _dot(below, eqf.astype(BF16)) + seen
            sc_scr[kb] = jnp.where((t >= hi) | (eq & (rank < need)), 0.0, NEG)
            return seen + jnp.sum(eqf, axis=0, keepdims=True)

        lax.fori_loop(0, nkb, sel_body, jnp.zeros((1, tq), F32))

    sc_scr[last] = jnp.where(causal_last, sc_scr[last], NEG)

    slopes = _slopes2(H_A)
    row = lax.broadcasted_iota(I32, (LANES, tq), 0)
    pos = _alibi_pos(tk)
    qts = []
    for h in range(H_A):
        pair, odd = h // 2, h % 2
        qp = qt_ref[0, pair * LANES:(pair + 1) * LANES, :]
        qts.append(jnp.concatenate(
            [jnp.where((row >= HD_A) == bool(odd), qp, jnp.zeros_like(qp)),
             _alibi_coef(slopes[h], tq)], axis=0))

    def scores(h, kb, is_last):
        cols = slice((h // 2) * LANES, (h // 2 + 1) * LANES)
        kblk = k_ref[0, pl.ds(pl.multiple_of(kb * tk, tk), tk), cols]
        return _dot(jnp.concatenate([kblk, pos], axis=1), qts[h]) + sc_scr[kb]

    def offset(h, kb):
        return slopes[h] * koff(kb).astype(F32)

    def values(h, kb):
        return vt_ref[0, kb, h * HD_A:(h + 1) * HD_A, :]

    ls, accs = _chain_pipeline(H_A, CHAINS_PER_STAGE, nkb, tq, HD_A, scores, offset, values,
                               s_scr, p_scr)
    for h in range(H_A):
        ot_scr[h * HD_A:(h + 1) * HD_A, :] = accs[h] / ls[h]
    o_ref[0] = ot_scr[...].T.astype(o_ref.dtype)


def _prompt_a(qat, ka, vta, qit, wt, ki, tq, tk, topk):
    b, t, _ = ka.shape
    nq, nk = t // tq, t // tk
    kern = functools.partial(_prompt_a_kernel, tq=tq, tk=tk, topk=topk)
    return pl.pallas_call(
        kern,
        grid=(b, nq),
        in_specs=[
            pl.BlockSpec((1, W_A, tq), lambda i, j: (i, 0, j)),
            pl.BlockSpec((1, t, W_A), lambda i, j: (i, 0, 0)),
            pl.BlockSpec((1, nk, W_A, tk), lambda i, j: (i, 0, 0, 0)),
            pl.BlockSpec((1, H_I * D_I, tq), lambda i, j: (i, 0, j)),
            pl.BlockSpec((1, H_I, tq), lambda i, j: (i, 0, j)),
            pl.BlockSpec((1, t, D_I), lambda i, j: (i, 0, 0)),
        ],
        out_specs=pl.BlockSpec((1, tq, W_A), lambda i, j: (i, j, 0)),
        out_shape=jax.ShapeDtypeStruct((b, t, W_A), F32),
        scratch_shapes=[
            pltpu.VMEM((nk, tk, tq), F32),
            pltpu.VMEM((CHAINS_PER_STAGE, nk, tk, tq), F32),
            pltpu.VMEM((CHAINS_PER_STAGE, nk, tk, tq), BF16),
            pltpu.VMEM((W_A, tq), F32),
        ],
        compiler_params=_cparams(("parallel", "arbitrary")),
        name="prompt_a",
    )(qat, ka, vta, qit, wt, ki)


def _lambda_of(lq1_ref, lk1_ref, lq2_ref, lk2_ref, lambda_init):
    s1 = jnp.sum(lq1_ref[...] * lk1_ref[...], axis=1, keepdims=True)
    s2 = jnp.sum(lq2_ref[...] * lk2_ref[...], axis=1, keepdims=True)
    return jnp.exp(s1) - jnp.exp(s2) + lambda_init


def _prompt_b_kernel(qt_ref, k_ref, vt_ref, lq1_ref, lk1_ref, lq2_ref, lk2_ref, g_ref, o_ref,
                     s_scr, p_scr, ot_scr, *, tq, tk, lambda_init):
    qi = pl.program_id(1)
    nkb, kl, ql, koff = _block_geometry(qi, tq, tk)
    causal_last = (kl + koff(nkb - 1)) <= ql
    lam = _lambda_of(lq1_ref, lk1_ref, lq2_ref, lk2_ref, lambda_init)
    slopes = _slopes2(H_B)
    row = lax.broadcasted_iota(I32, (LANES, tq), 0)
    pos = _alibi_pos(tk)
    qts = []
    for h in range(H_B):
        qp = qt_ref[0, h * LANES:(h + 1) * LANES, :]
        coef = _alibi_coef(slopes[h], tq)
        qts.append(jnp.concatenate([jnp.where(row < HD_B, qp, jnp.zeros_like(qp)), coef], axis=0))
        qts.append(jnp.concatenate([jnp.where(row >= HD_B, qp, jnp.zeros_like(qp)), coef], axis=0))

    def scores(c, kb, is_last):
        h = c // 2
        kblk = k_ref[0, pl.ds(pl.multiple_of(kb * tk, tk), tk), h * LANES:(h + 1) * LANES]
        s = _dot(jnp.concatenate([kblk, pos], axis=1), qts[c])
        return jnp.where(causal_last, s, NEG) if is_last else s

    def offset(c, kb):
        return slopes[c // 2] * koff(kb).astype(F32)

    def values(c, kb):
        h = c // 2
        return vt_ref[0, kb, h * LANES:(h + 1) * LANES, :]

    ls, accs = _chain_pipeline(2 * H_B, CHAINS_PER_STAGE, nkb, tq, LANES, scores, offset, values,
                               s_scr, p_scr, peel_last=True)
    g = g_ref[...]
    for h in range(H_B):
        o = accs[2 * h] / ls[2 * h] - lam * (accs[2 * h + 1] / ls[2 * h + 1])
        ms = jnp.mean(o * o, axis=0, keepdims=True)
        o = o * lax.rsqrt(ms + RMS_EPS) * g
        ot_scr[h * LANES:(h + 1) * LANES, :] = o * (1.0 - lambda_init)
    o_ref[0] = ot_scr[...].T.astype(o_ref.dtype)


def _prompt_b(qbt, kb, vtb, lq1, lk1, lq2, lk2, g_col, tq, tk, lambda_init):
    b, t, _ = kb.shape
    nq, nk = t // tq, t // tk
    kern = functools.partial(_prompt_b_kernel, tq=tq, tk=tk, lambda_init=lambda_init)
    small = lambda a: pl.BlockSpec(a.shape, lambda i, j: (0, 0))
    return pl.pallas_call(
        kern,
        grid=(b, nq),
        in_specs=[
            pl.BlockSpec((1, W_B, tq), lambda i, j: (i, 0, j)),
            pl.BlockSpec((1, t, W_B), lambda i, j: (i, 0, 0)),
            pl.BlockSpec((1, nk, W_B, tk), lambda i, j: (i, 0, 0, 0)),
            small(lq1), small(lk1), small(lq2), small(lk2), small(g_col),
        ],
        out_specs=pl.BlockSpec((1, tq, W_B), lambda i, j: (i, j, 0)),
        out_shape=jax.ShapeDtypeStruct((b, t, W_B), F32),
        scratch_shapes=[
            pltpu.VMEM((CHAINS_PER_STAGE, nk, tk, tq), F32),
            pltpu.VMEM((CHAINS_PER_STAGE, nk, tk, tq), BF16),
            pltpu.VMEM((W_B, tq), F32),
        ],
        compiler_params=_cparams(("parallel", "arbitrary")),
        name="prompt_b",
    )(qbt, kb, vtb, lq1, lk1, lq2, lk2, g_col)


def _sigmoid(x):
    return 1.0 / (1.0 + jnp.exp(-x))


def _merge_kernel(x_ref, oa_ref, ob_ref, wz_ref, bz_ref, wpa_ref, wpb_ref, wout_ref,
                  lng_ref, lnb_ref, y_ref, *, alpha):
    x = x_ref[...]
    xh = x.astype(BF16)

    def proj(c0, n):
        return _dot_nt(xh, wz_ref[c0:c0 + n, :]) + bz_ref[:, c0:c0 + n]

    d = x.shape[1]
    za = proj(0, W_A)
    ua = (oa_ref[...] * (za * _sigmoid(za))).astype(BF16)
    ya = _dot(ua, wpa_ref[...])
    zb = proj(W_A, W_B)
    ub = (ob_ref[...] * (zb * _sigmoid(zb))).astype(BF16)
    yb = _dot(ub, wpb_ref[...])
    m = _sigmoid(proj(W_A + W_B, d)) * ya + _sigmoid(proj(W_A + W_B + d, d)) * yb
    r = alpha * x + _dot(m.astype(BF16), wout_ref[...])
    mu = jnp.mean(r, axis=1, keepdims=True)
    rc = r - mu
    var = jnp.mean(rc * rc, axis=1, keepdims=True)
    y_ref[...] = rc * lax.rsqrt(var + LN_EPS) * lng_ref[...] + lnb_ref[...]


def _merge(x2d, oa, ob, wz, bz, wpa, wpb, wout, lng, lnb, tm, alpha):
    n, d = x2d.shape
    row = lambda w: pl.BlockSpec((tm, w), lambda i: (i, 0))
    full = lambda a: pl.BlockSpec(a.shape, lambda i: (0, 0))
    return pl.pallas_call(
        functools.partial(_merge_kernel, alpha=alpha),
        grid=(n // tm,),
        in_specs=[row(d), row(W_A), row(W_B), full(wz), full(bz), full(wpa), full(wpb),
                  full(wout), full(lng), full(lnb)],
        out_specs=row(d),
        out_shape=jax.ShapeDtypeStruct((n, d), F32),
        compiler_params=_cparams(("parallel",)),
        name="merge",
    )(x2d, oa, ob, wz, bz, wpa, wpb, wout, lng, lnb)


def _page_specs(n, shape, first_page):
    def spec(j):
        return pl.BlockSpec((1,) + shape, lambda i, s, pt, j=j: (pt[i, first_page(s) + j], 0, 0))
    return [spec(j) for j in range(n)]


def _sample_idx_kernel(pt_ref, qi_ref, wcol_ref, kin_ref, *rest, n_pages, pps, n_tok):
    kip_refs, sc_ref = rest[:pps], rest[pps]
    step = pl.program_id(1)
    n_steps = n_pages // pps
    lane = lax.broadcasted_iota(I32, (SUBLANES, PAGE), 1)
    trow = lax.broadcasted_iota(I32, (SUBLANES, PAGE), 0)

    def scores(kpages_t):
        r = _dot(qi_ref[0], kpages_t.astype(BF16))
        r = jnp.maximum(r, 0.0) * wcol_ref[0]
        per_tok = [jnp.sum(r[t * H_I:(t + 1) * H_I, :], axis=0, keepdims=True)
                   for t in range(n_tok)]
        pad = [jnp.zeros_like(per_tok[0])] * (SUBLANES - n_tok)
        return jnp.concatenate(per_tok + pad, axis=0)

    s_all = scores(jnp.concatenate([r[0] for r in kip_refs], axis=1))
    for j in range(pps):
        sc_ref[0, step * pps + j] = s_all[:, j * PAGE:(j + 1) * PAGE]

    @pl.when(step == n_steps - 1)
    def _():
        s = jnp.where(lane <= trow, scores(kin_ref[0]), NEG)
        sc_ref[0, n_pages] = jnp.where(lane < n_tok, s, PAD_SCORE)


def _sample_select_kernel(sc_ref, bias_ref, *, n_tok, topk):
    nb, n_tiles = sc_ref.shape[0], sc_ref.shape[1]
    rows = nb * SUBLANES
    lane = lax.broadcasted_iota(I32, (rows, PAGE), 1)
    trow = lax.broadcasted_iota(I32, (rows, PAGE), 0) % SUBLANES
    k = float(topk)
    tile = lambda j: sc_ref[:, j].reshape(rows, PAGE)

    def count_ge(v):
        cnt = jnp.zeros((rows, PAGE), F32)
        for j in range(n_tiles):
            cnt = jnp.where(tile(j) >= v, cnt + 1.0, cnt)
        return jnp.sum(cnt, axis=1, keepdims=True)

    lo, hi = _kth_largest_cell(count_ge, (rows, 1), k)
    need = k - count_ge(hi)
    before = (lax.broadcasted_iota(I32, (PAGE, PAGE), 0)
              < lax.broadcasted_iota(I32, (PAGE, PAGE), 1)).astype(BF16)
    seen = jnp.zeros((rows, 1), F32)
    for j in range(n_tiles):
        t = tile(j)
        eq = (t >= lo) & (t < hi)
        eqf = jnp.where(eq, 1.0, 0.0)
        rank = _dot(eqf.astype(BF16), before) + seen
        sel = (t >= hi) | (eq & (rank < need))
        if j == n_tiles - 1:
            sel = sel & (lane <= trow) & (lane < n_tok)
        bias_ref[:, j] = jnp.where(sel, 0.0, NEG).reshape(nb, SUBLANES, PAGE)
        seen = seen + jnp.sum(eqf, axis=1, keepdims=True)


def _sample_select(scores, n_tok, topk):
    return pl.pallas_call(
        functools.partial(_sample_select_kernel, n_tok=n_tok, topk=topk),
        out_shape=jax.ShapeDtypeStruct(scores.shape, F32),
        compiler_params=pltpu.CompilerParams(vmem_limit_bytes=VMEM_LIMIT),
        name="sample_select",
    )(scores)


def _sample_idx(page_table, qi_rows, wcol, cache_ki_t, ki_new_t, n_tok):
    b, n_pages = page_table.shape
    rows = n_tok * H_I
    pps = min(IDX_PAGES_PER_STEP, n_pages)
    kern = functools.partial(_sample_idx_kernel, n_pages=n_pages, pps=pps, n_tok=n_tok)
    grid_spec = pltpu.PrefetchScalarGridSpec(
        num_scalar_prefetch=1,
        grid=(b, n_pages // pps),
        in_specs=[
            pl.BlockSpec((1, rows, D_I), lambda i, s, pt: (i, 0, 0)),
            pl.BlockSpec((1, rows, 1), lambda i, s, pt: (i, 0, 0)),
            pl.BlockSpec((1, D_I, PAGE), lambda i, s, pt: (i, 0, 0)),
        ] + _page_specs(pps, (D_I, PAGE), lambda s: s * pps),
        out_specs=pl.BlockSpec((1, n_pages + 1, SUBLANES, PAGE), lambda i, s, pt: (i, 0, 0, 0)),
    )
    return pl.pallas_call(
        kern,
        grid_spec=grid_spec,
        out_shape=jax.ShapeDtypeStruct((b, n_pages + 1, SUBLANES, PAGE), F32),
        compiler_params=_cparams(("parallel", "arbitrary")),
        name="sample_idx",
    )(page_table, qi_rows, wcol, ki_new_t, *([cache_ki_t] * pps))


def _softmax_update(s, m_scr, l_scr):
    m = m_scr[...]
    m_new = jnp.maximum(m, jnp.max(s, axis=1, keepdims=True))
    p = jnp.exp2(s - m_new)
    alpha = jnp.exp2(m - m_new)
    l_scr[...] = alpha * l_scr[...] + jnp.sum(p, axis=1, keepdims=True)
    m_scr[...] = m_new
    return p, alpha


def _decode_a_kernel(pt_ref, wq_ref, slope_ref, kn_ref, vn_ref, bias_ref, biasn_ref, *rest,
                     n_pages, pps, n_tok, past_len):
    kp_refs, vp_refs = rest[:pps], rest[pps:2 * pps]
    o_ref, acc_scr, m_scr, l_scr = rest[2 * pps:]
    step = pl.program_id(1)
    n_steps = n_pages // pps
    rows = n_tok * H_A

    @pl.when(step == 0)
    def _():
        acc_scr[...] = jnp.zeros_like(acc_scr)
        m_scr[...] = jnp.full_like(m_scr, NEG)
        l_scr[...] = jnp.zeros_like(l_scr)

    def update(k_ts, v_ts, first_pos, b8s, new_keys):
        n = len(k_ts)
        lane = lax.broadcasted_iota(I32, (rows, n * PAGE), 1)
        trow = lax.broadcasted_iota(I32, (rows, n * PAGE), 0) // H_A
        s = jnp.concatenate([_dot(wq_ref[0], k_t.astype(BF16)) for k_t in k_ts], axis=1)
        s = s - slope_ref[...] * (past_len + trow - (first_pos + lane)).astype(F32)
        s = s + jnp.concatenate(
            [jnp.broadcast_to(jnp.concatenate([b8[t:t + 1, :] for b8 in b8s], axis=1),
                              (H_A, n * PAGE)) for t in range(n_tok)], axis=0)
        if new_keys:
            s = jnp.where((lane <= trow) & (lane < n_tok), s, NEG)
        p, alpha = _softmax_update(s, m_scr, l_scr)
        p = p.astype(BF16)
        pv = _dot_nt(p[:, :PAGE], v_ts[0].astype(BF16))
        for j in range(1, n):
            pv = pv + _dot_nt(p[:, j * PAGE:(j + 1) * PAGE], v_ts[j].astype(BF16))
        acc_scr[...] = acc_scr[...] * alpha + pv

    update([r[0] for r in kp_refs], [r[0] for r in vp_refs], step * (pps * PAGE),
           [bias_ref[0, j] for j in range(pps)], False)

    @pl.when(step == n_steps - 1)
    def _():
        update([kn_ref[0]], [vn_ref[0]], past_len, [biasn_ref[0, 0]], True)
        o = acc_scr[...] / l_scr[...]
        own = (lax.broadcasted_iota(I32, (rows, W_A), 1) // HD_A
               == lax.broadcasted_iota(I32, (rows, W_A), 0) % H_A)
        o = jnp.where(own, o, 0.0)
        o_ref[0] = jnp.concatenate(
            [jnp.sum(o[t * H_A:(t + 1) * H_A, :], axis=0, keepdims=True) for t in range(n_tok)],
            axis=0)


def _decode_a(page_table, wq, slope_col, cache_k_t, cache_v_t, k_new_t, v_new_t, bias, n_tok,
              past_len):
    b, n_pages = page_table.shape
    rows = n_tok * H_A
    pps = min(PAGES_PER_STEP, n_pages)
    kern = functools.partial(_decode_a_kernel, n_pages=n_pages, pps=pps, n_tok=n_tok,
                             past_len=past_len)
    pages = _page_specs(pps, (W_A, PAGE), lambda s: s * pps)
    per_batch = lambda shape: pl.BlockSpec((1,) + shape, lambda i, s, pt: (i,) + (0,) * len(shape))
    grid_spec = pltpu.PrefetchScalarGridSpec(
        num_scalar_prefetch=1,
        grid=(b, n_pages // pps),
        in_specs=[
            per_batch((rows, W_A)),
            pl.BlockSpec((rows, 1), lambda i, s, pt: (0, 0)),
            per_batch((W_A, PAGE)), per_batch((W_A, PAGE)),
            pl.BlockSpec((1, pps, SUBLANES, PAGE), lambda i, s, pt: (i, s, 0, 0)),
            pl.BlockSpec((1, 1, SUBLANES, PAGE), lambda i, s, pt: (i, n_pages, 0, 0)),
        ] + pages + pages,
        out_specs=per_batch((n_tok, W_A)),
        scratch_shapes=[pltpu.VMEM((rows, W_A), F32), pltpu.VMEM((rows, 1), F32),
                        pltpu.VMEM((rows, 1), F32)],
    )
    return pl.pallas_call(
        kern,
        grid_spec=grid_spec,
        out_shape=jax.ShapeDtypeStruct((b, n_tok, W_A), F32),
        compiler_params=_cparams(("parallel", "arbitrary")),
        name="decode_a",
    )(page_table, wq, slope_col, k_new_t, v_new_t, bias, bias,
      *([cache_k_t] * pps), *([cache_v_t] * pps))


def _decode_b_kernel(pt_ref, wq_ref, kn_ref, vn_ref, lq1_ref, lk1_ref, lq2_ref, lk2_ref, g_ref,
                     *rest, n_pages, pps, n_tok, past_len, lambda_init):
    kp_refs, vp_refs = rest[:pps], rest[pps:2 * pps]
    o_ref, acc_scr, m_scr, l_scr = rest[2 * pps:]
    step = pl.program_id(1)
    n_steps = n_pages // pps
    hrows = 2 * n_tok
    rows = H_B * hrows
    slope_col = jnp.concatenate([jnp.full((hrows, 1), s, F32) for s in _slopes2(H_B)], axis=0)

    @pl.when(step == 0)
    def _():
        acc_scr[...] = jnp.zeros_like(acc_scr)
        m_scr[...] = jnp.full_like(m_scr, NEG)
        l_scr[...] = jnp.zeros_like(l_scr)

    def update(k_hs, v_hs, first_pos, new_keys):
        n = len(k_hs[0])
        lane = lax.broadcasted_iota(I32, (rows, n * PAGE), 1)
        trow = lax.broadcasted_iota(I32, (rows, n * PAGE), 0) % n_tok
        s = jnp.concatenate(
            [jnp.concatenate([_dot_nt(wq_ref[0, h], k.astype(BF16)) for k in k_hs[h]], axis=1)
             for h in range(H_B)], axis=0)
        s = s - slope_col * (past_len + trow - (first_pos + lane)).astype(F32)
        if new_keys:
            s = jnp.where((lane <= trow) & (lane < n_tok), s, NEG)
        p, alpha = _softmax_update(s, m_scr, l_scr)
        p = p.astype(BF16)
        pvs = []
        for h in range(H_B):
            ph = p[h * hrows:(h + 1) * hrows, :]
            pv = _dot(ph[:, :PAGE], v_hs[h][0].astype(BF16))
            for j in range(1, n):
                pv = pv + _dot(ph[:, j * PAGE:(j + 1) * PAGE], v_hs[h][j].astype(BF16))
            pvs.append(pv)
        acc_scr[...] = acc_scr[...] * alpha + jnp.concatenate(pvs, axis=0)

    head_rows = lambda refs, h: [r[0, pl.ds(h, PAGE, stride=H_B), :] for r in refs]
    update([head_rows(kp_refs, h) for h in range(H_B)],
           [head_rows(vp_refs, h) for h in range(H_B)], step * (pps * PAGE), False)

    @pl.when(step == n_steps - 1)
    def _():
        update([[kn_ref[0, h]] for h in range(H_B)], [[vn_ref[0, h]] for h in range(H_B)],
               past_len, True)
        lam = _lambda_of(lq1_ref, lk1_ref, lq2_ref, lk2_ref, lambda_init)
        o_all = acc_scr[...] / l_scr[...]
        for h in range(H_B):
            o = (o_all[h * hrows:h * hrows + n_tok, :]
                 - lam * o_all[h * hrows + n_tok:(h + 1) * hrows, :])
            ms = jnp.mean(o * o, axis=1, keepdims=True)
            o = o * lax.rsqrt(ms + RMS_EPS) * g_ref[...] * (1.0 - lambda_init)
            o_ref[0, :, h * LANES:(h + 1) * LANES] = o


def _decode_b(page_table, wq, cache_k_r, cache_v_r, k_new, v_new, lam_args, g_row, n_tok,
              past_len, lambda_init):
    b, n_pages = page_table.shape
    rows = 2 * n_tok
    pps = min(PAGES_PER_STEP, n_pages)
    kern = functools.partial(_decode_b_kernel, n_pages=n_pages, pps=pps, n_tok=n_tok,
                             past_len=past_len, lambda_init=lambda_init)
    pages = _page_specs(pps, (PAGE * H_B, 2 * HD_B), lambda s: s * pps)
    per_batch = lambda shape: pl.BlockSpec((1,) + shape, lambda i, s, pt: (i,) + (0,) * len(shape))
    small = lambda a: pl.BlockSpec(a.shape, lambda i, s, pt: (0, 0))
    grid_spec = pltpu.PrefetchScalarGridSpec(
        num_scalar_prefetch=1,
        grid=(b, n_pages // pps),
        in_specs=[
            per_batch((H_B, rows, 2 * HD_B)),
            per_batch((H_B, PAGE, 2 * HD_B)), per_batch((H_B, PAGE, 2 * HD_B)),
        ] + [small(a) for a in lam_args] + [small(g_row)] + pages + pages,
        out_specs=per_batch((n_tok, W_B)),
        scratch_shapes=[pltpu.VMEM((H_B * rows, 2 * HD_B), F32), pltpu.VMEM((H_B * rows, 1), F32),
                        pltpu.VMEM((H_B * rows, 1), F32)],
    )
    return pl.pallas_call(
        kern,
        grid_spec=grid_spec,
        out_shape=jax.ShapeDtypeStruct((b, n_tok, W_B), F32),
        compiler_params=_cparams(("parallel", "arbitrary")),
        name="decode_b",
    )(page_table, wq, k_new, v_new, *lam_args, g_row,
      *([cache_k_r] * pps), *([cache_v_r] * pps))


def _split_rows(wt, d_model):
    out, off = {}, 0
    for name, n in SPLITS:
        n = d_model if n is None else n
        out[name] = wt[off:off + n]
        off += n
    return out


def _pad_rows(a, rows):
    return jnp.pad(a, [(0, rows - a.shape[0])] + [(0, 0)] * (a.ndim - 1))


def _proj_weights(w_in, b_in, d_model):
    w = _split_rows(w_in.T, d_model)
    b = _split_rows(b_in, d_model)
    order1 = ['q_a', 'k_a', 'v_a', 'q_b', 'k_b', 'v_b', 'q_idx']
    group1 = lambda p: jnp.concatenate(
        [p[n] for n in order1] + [_pad_rows(p['k_idx'], LANES), _pad_rows(p['w_idx'], LANES)])
    order2 = ['z_a', 'z_b', 'g_a', 'g_b']
    group2 = lambda p: jnp.concatenate([p[n] for n in order2])
    return (group1(w).astype(BF16), group1(b)[None, :], group2(w).astype(BF16),
            group2(b)[None, :])


def _prompt_branches(xp, bp, t, w1t, b1, lam_args, g, tq, tk, topk, lambda_init):
    (qat, kat32, ka16, vat32, vat, qbt, kb32, kb16, vb32, vbt, qit, kit32, ki16, wit) = _project(
        xp, w1t, b1, tk, batch=bp)
    r3 = lambda a: a.reshape(bp, t, a.shape[-1])
    o_a = _prompt_a(qat, r3(ka16), vat, qit, wit, r3(ki16), tq, tk, topk)
    o_b = _prompt_b(qbt, r3(kb16), vbt, *lam_args, g[:, None], tq, tk, lambda_init)
    kv = (kat32.reshape(bp, H_A, HD_A, t).transpose(0, 3, 1, 2),
          vat32.reshape(bp, H_A, HD_A, t).transpose(0, 3, 1, 2),
          kit32.transpose(0, 2, 1),
          kb32.reshape(bp, t, H_B, 2 * HD_B), vb32.reshape(bp, t, H_B, 2 * HD_B))
    return o_a.reshape(bp * t, W_A), o_b.reshape(bp * t, W_B), kv


def _sample_branches(xs, bs, ts, page_table, c_k_a, c_v_a, c_k_idx, c_k_b, c_v_b, w1t, b1,
                     lam_args, g, topk, lambda_init):
    n_pool = c_k_a.shape[0]
    past_len = page_table.shape[1] * PAGE
    qa, ka32, va32, qb, kb32, vb32, qidx, ki32, wi = _project(xs, w1t, b1, bs * ts)
    s3 = lambda a: a.reshape(bs, ts, a.shape[-1])
    new_t = lambda a: jnp.pad(s3(a).transpose(0, 2, 1), ((0, 0), (0, 0), (0, PAGE - ts)))
    qi_rows = s3(qidx).reshape(bs, ts * H_I, D_I)
    wcol = s3(wi).reshape(bs, ts * H_I, 1)
    scores = _sample_idx(page_table, qi_rows, wcol, c_k_idx.transpose(0, 2, 1), new_t(ki32), ts)
    bias = _sample_select(scores, ts, topk)
    eye_a = jnp.eye(H_A, dtype=BF16)
    wq_a = (s3(qa).reshape(bs, ts, H_A, 1, HD_A) * eye_a[None, None, :, :, None]
            ).reshape(bs, ts * H_A, W_A)
    slope_a = jnp.tile(jnp.asarray(_slopes2(H_A), F32), ts)[:, None]
    page_t = lambda c: c.transpose(0, 2, 3, 1).reshape(n_pool, W_A, PAGE)
    o_a = _decode_a(page_table, wq_a, slope_a, page_t(c_k_a), page_t(c_v_a), new_t(ka32),
                    new_t(va32), bias, ts, past_len)
    qb4 = s3(qb).reshape(bs, ts, H_B, 2 * HD_B).transpose(0, 2, 1, 3)
    first = (jnp.arange(2 * HD_B) < HD_B)
    wq_b = jnp.concatenate([jnp.where(first, qb4, 0), jnp.where(first, 0, qb4)], axis=2)
    new_h = lambda a: jnp.pad(a.reshape(bs, ts, H_B, 2 * HD_B).transpose(0, 2, 1, 3),
                              ((0, 0), (0, 0), (0, PAGE - ts), (0, 0)))
    page_r = lambda c: c.reshape(n_pool, PAGE * H_B, 2 * HD_B)
    o_b = _decode_b(page_table, wq_b, page_r(c_k_b), page_r(c_v_b), new_h(kb32), new_h(vb32),
                    lam_args, g[None, :], ts, past_len, lambda_init)
    return (o_a.reshape(bs * ts, W_A), o_b.reshape(bs * ts, W_B),
            (ka32, va32, ki32, kb32, vb32))


def kernel(x_prompt, x_sample, cache_k_a, cache_v_a, cache_k_idx, cache_k_b, cache_v_b, page_table,
           w_in, b_in, lambda_q1, lambda_k1, lambda_q2, lambda_k2, subln_g, w_pa, w_pb, w_out,
           ln_g, ln_b):
    depth = w_in.shape[0]
    bp, t, d = x_prompt.shape
    bs, ts, _ = x_sample.shape
    past_len = page_table.shape[1] * PAGE
    alpha = (2.0 * depth) ** 0.25
    tq = min(256, t)
    tk = min(512, t)
    tm = min(512, bp * t)
    topk_p = min(TOPK_MAX, t // 4)
    topk_s = min(TOPK_MAX, (past_len + ts) // 4)

    xp = x_prompt.reshape(bp * t, d)
    xs = x_sample.reshape(bs * ts, d)
    pk, sk = [], []
    for l in range(depth):
        lambda_init = 0.8 - 0.6 * math.exp(-0.3 * l)
        w1t, b1, wzt, bz = _proj_weights(w_in[l], b_in[l], d)
        lam_args = (lambda_q1[l][None, :], lambda_k1[l][None, :],
                    lambda_q2[l][None, :], lambda_k2[l][None, :])
        merge_w = (wzt, bz, w_pa[l].astype(BF16), w_pb[l].astype(BF16), w_out[l].astype(BF16),
                   ln_g[l][None, :], ln_b[l][None, :])
        o_a, o_b, kv = _prompt_branches(xp, bp, t, w1t, b1, lam_args, subln_g[l], tq, tk, topk_p,
                                        lambda_init)
        pk.append(kv)
        xp = _merge(xp, o_a, o_b, *merge_w, tm, alpha)
        o_a, o_b, kv = _sample_branches(xs, bs, ts, page_table, cache_k_a[l], cache_v_a[l],
                                        cache_k_idx[l], cache_k_b[l], cache_v_b[l], w1t, b1,
                                        lam_args, subln_g[l], topk_s, lambda_init)
        sk.append(kv)
        xs = _merge(xs, o_a, o_b, *merge_w, bs * ts, alpha)

    def stacked(layers, b, n):
        shapes = ((b, n, H_A, HD_A), (b, n, H_A, HD_A), (b, n, D_I),
                  (b, n, H_B, 2 * HD_B), (b, n, H_B, 2 * HD_B))
        return tuple(jnp.stack([kv[i].reshape(s) for kv in layers]) for i, s in enumerate(shapes))

    return (xp.reshape(bp, t, d), xs.reshape(bs, ts, d)) + stacked(pk, bp, t) + stacked(sk, bs, ts)
```

```python
import functools
import math

import jax
import jax.numpy as jnp
import numpy as np
from jax import lax
from jax.experimental import pallas as pl
from jax.experimental.pallas import tpu as pltpu

F32 = jnp.float32
BF16 = jnp.bfloat16
I32 = jnp.int32

H_A, HD_A = 8, 64
H_I, D_I = 8, 32
H_B, HD_B = 4, 64
TOPK_MAX = 256
PAGE = 128
LN_EPS = 1e-5
RMS_EPS = 1e-5
NEG = -1e30
PAD_SCORE = -3e38
LOG2E = math.log2(math.e)

W_A = H_A * HD_A
W_B = H_B * 2 * HD_B
LANES = 128
SUBLANES = 8
VMEM_LIMIT = 56 * 1024 * 1024
TIE_REFINE_STEPS = 16
PAGES_PER_STEP = 16
IDX_PAGES_PER_STEP = 64
CHAINS_PER_STAGE = 2
PART_ROWS = 64

SPLITS = (
    ('q_a', W_A), ('k_a', W_A), ('v_a', W_A),
    ('q_idx', H_I * D_I), ('k_idx', D_I), ('w_idx', H_I), ('z_a', W_A),
    ('q_b', W_B), ('k_b', W_B), ('v_b', W_B), ('z_b', W_B),
    ('g_a', None), ('g_b', None),
)


def _slopes2(n):
    return tuple(LOG2E * 2.0 ** (-8.0 * (i + 1) / n) for i in range(n))


def _cparams(sem):
    return pltpu.CompilerParams(dimension_semantics=sem, vmem_limit_bytes=VMEM_LIMIT)


def _dot_nt(a, b):
    return lax.dot_general(a, b, (((1,), (1,)), ((), ())), preferred_element_type=F32)


def _dot(a, b):
    return jnp.dot(a, b, preferred_element_type=F32)


_C_QA, _C_KA, _C_VA, _C_QB, _C_KB, _C_VB = 0, 512, 1024, 1536, 2048, 2560
_C_QI, _C_KI, _C_WI, _C_END = 3072, 3328, 3456, 3584


def _proj_kernel(x_ref, w_ref, b_ref, *out_refs, key_major):
    x = x_ref[...].astype(BF16)

    def proj(c0, n):
        return _dot_nt(x, w_ref[c0:c0 + n, :]) + b_ref[:, c0:c0 + n]

    qa = proj(_C_QA, W_A) * (LOG2E * HD_A ** -0.5)
    ka, va = proj(_C_KA, W_A), proj(_C_VA, W_A)
    qb = proj(_C_QB, W_B) * (LOG2E * HD_B ** -0.5)
    kb, vb = proj(_C_KB, W_B), proj(_C_VB, W_B)
    qi = proj(_C_QI, H_I * D_I)
    ki = proj(_C_KI, LANES)
    wi = proj(_C_WI, LANES) * ((H_I ** -0.5) * (D_I ** -0.5))
    if key_major:
        (qat_ref, kat32_ref, ka16_ref, vat32_ref, vat_ref, qbt_ref, kb32_ref, kb16_ref, vb32_ref,
         vbt_ref, qit_ref, kit32_ref, ki16_ref, wit_ref) = out_refs
        qat_ref[0] = qa.T.astype(BF16)
        qbt_ref[0] = qb.T.astype(BF16)
        kat32_ref[0] = ka.T
        ka16_ref[...] = ka.astype(BF16)
        kb16_ref[...] = kb.astype(BF16)
        va_t, vb_t = va.T, vb.T
        vat32_ref[0] = va_t
        vat_ref[0, 0] = va_t.astype(BF16)
        vbt_ref[0, 0] = vb_t.astype(BF16)
        qit_ref[0] = qi.T.astype(BF16)
        kit32_ref[0] = ki.T[:D_I, :]
        ki16_ref[...] = ki[:, :D_I].astype(BF16)
        wit_ref[0] = wi.T[:H_I, :]
    else:
        qa_ref, ka32_ref, va32_ref, qb_ref, kb32_ref, vb32_ref, qi_ref, ki32_ref, wi_ref = out_refs
        qa_ref[...] = qa.astype(BF16)
        qb_ref[...] = qb.astype(BF16)
        ka32_ref[...] = ka
        va32_ref[...] = va
        qi_ref[...] = qi.astype(BF16)
        ki32_ref[...] = ki[:, :D_I]
        wi_ref[...] = wi[:, :H_I]
    hw = 2 * HD_B
    for h in range(H_B):
        rows_h = pl.ds(h, x.shape[0], stride=H_B)
        kb32_ref[rows_h, :] = kb[:, h * hw:(h + 1) * hw]
        vb32_ref[rows_h, :] = vb[:, h * hw:(h + 1) * hw]


def _project(x2d, w1t, b1, tm, batch=None):
    n, d = x2d.shape
    sds = jax.ShapeDtypeStruct
    row = lambda w, dt: (pl.BlockSpec((tm, w), lambda i: (i, 0)), sds((n, w), dt))
    full = lambda a: pl.BlockSpec(a.shape, lambda i: (0, 0))
    head_rows = (pl.BlockSpec((tm * H_B, 2 * HD_B), lambda i: (i, 0)),
                 sds((n * H_B, 2 * HD_B), F32))
    if batch is None:
        outs = [row(W_A, BF16), row(W_A, F32), row(W_A, F32), row(W_B, BF16), head_rows,
                head_rows, row(H_I * D_I, BF16), row(D_I, F32), row(H_I, F32)]
    else:
        t = n // batch
        nk = t // tm
        blocks = lambda w: (pl.BlockSpec((1, 1, w, tm), lambda i: (i // nk, i % nk, 0, 0)),
                            sds((batch, nk, w, tm), BF16))
        feat = lambda f, dt: (pl.BlockSpec((1, f, tm), lambda i: (i // nk, 0, i % nk)),
                              sds((batch, f, t), dt))
        outs = [feat(W_A, BF16), feat(W_A, F32), row(W_A, BF16), feat(W_A, F32), blocks(W_A),
                feat(W_B, BF16), head_rows, row(W_B, BF16), head_rows, blocks(W_B),
                feat(H_I * D_I, BF16), feat(D_I, F32), row(D_I, BF16), feat(H_I, F32)]
    return pl.pallas_call(
        functools.partial(_proj_kernel, key_major=batch is not None),
        grid=(n // tm,),
        in_specs=[pl.BlockSpec((tm, d), lambda i: (i, 0)), full(w1t), full(b1)],
        out_specs=[o[0] for o in outs],
        out_shape=[o[1] for o in outs],
        compiler_params=_cparams(("parallel",)),
        name="proj",
    )(x2d, w1t, b1)


def _float_of_key(u):
    s = u ^ jnp.int32(-2 ** 31)
    fb = jnp.where(s >= 0, s, s ^ jnp.int32(0x7FFFFFFF))
    return lax.bitcast_convert_type(fb, F32)


def _kth_largest_cell(count_ge, shape, k):
    def bit_body(i, prefix):
        bit = lax.shift_left(jnp.int32(1), jnp.int32(31) - i)
        cand = prefix | bit
        return jnp.where(count_ge(_float_of_key(cand)) >= k, cand, prefix)

    prefix = lax.fori_loop(0, 32, bit_body, jnp.zeros(shape, I32))

    def refine(_, c):
        lo, hi = c
        mid = lo + (hi - lo) * 0.5
        ok = count_ge(mid) >= k
        return jnp.where(ok, mid, lo), jnp.where(ok, hi, mid)

    return lax.fori_loop(0, TIE_REFINE_STEPS, refine,
                         (_float_of_key(prefix), _float_of_key(prefix + 1)))


def _fold_rows(x, op):
    out = x[:PART_ROWS]
    for i in range(1, x.shape[0] // PART_ROWS):
        out = op(out, x[i * PART_ROWS:(i + 1) * PART_ROWS])
    return out


def _split3(x):
    out = []
    for _ in range(3):
        part = float(np.asarray(x, np.float32).astype(BF16))
        out.append(part)
        x -= part
    return out


def _alibi_pos(tk):
    kl = lax.broadcasted_iota(I32, (tk, LANES), 0)
    col = lax.broadcasted_iota(I32, (tk, LANES), 1)
    hi = lax.shift_right_logical(kl, 6)
    lo = kl & 63
    return jnp.where(col < 3, hi, jnp.where(col < 6, lo, 0)).astype(F32).astype(BF16)


def _alibi_coef(slope, tq):
    row = lax.broadcasted_iota(I32, (LANES, tq), 0)
    coef = jnp.zeros((LANES, tq), F32)
    for i, v in enumerate(_split3(64.0 * slope) + _split3(slope)):
        coef = jnp.where(row == i, v, coef)
    return coef.astype(BF16)


def _chain_pipeline(n_chain, g, nkb, tq, acc_rows, scores, offset, values, s_scr, p_scr,
                    peel_last=False):
    n_group = n_chain // g
    neg0 = jnp.full((PART_ROWS, tq), NEG, F32)
    zero = jnp.zeros((PART_ROWS, tq), F32)
    zacc = jnp.zeros((acc_rows, tq), F32)
    ms, ls, accs = [None] * n_chain, [None] * n_chain, [None] * n_chain
    for grp in range(n_group + 2):
        def body(kb, carry, grp=grp, is_last=False):
            m_a, l_b, acc_c = list(carry[0]), list(carry[1]), list(carry[2])
            for i in range(g):
                if grp >= 2:
                    c = (grp - 2) * g + i
                    acc_c[i] = acc_c[i] + _dot(values(c, kb), p_scr[i, kb])
                if 1 <= grp <= n_group:
                    c = (grp - 1) * g + i
                    p = jnp.exp2(s_scr[i, kb] - (ms[c] - offset(c, kb)))
                    p_scr[i, kb] = p.astype(BF16)
                    l_b[i] = l_b[i] + _fold_rows(p, jnp.add)
                if grp < n_group:
                    c = grp * g + i
                    s = scores(c, kb, is_last)
                    s_scr[i, kb] = s
                    m_a[i] = jnp.maximum(m_a[i], _fold_rows(s, jnp.maximum) + offset(c, kb))
            return tuple(m_a), tuple(l_b), tuple(acc_c)

        n_loop = nkb - 1 if peel_last else nkb
        carry = lax.fori_loop(0, n_loop // 2,
                              lambda j, c, body=body: body(2 * j + 1, body(2 * j, c)),
                              ((neg0,) * g, (zero,) * g, (zacc,) * g))
        carry = lax.cond(n_loop % 2 == 1, lambda c, body=body: body(n_loop - 1, c),
                         lambda c: c, carry)
        m_a, l_b, acc_c = body(nkb - 1, carry, is_last=True) if peel_last else carry
        for i in range(g):
            if grp < n_group:
                ms[grp * g + i] = jnp.max(m_a[i], axis=0, keepdims=True)
            if 1 <= grp <= n_group:
                ls[(grp - 1) * g + i] = jnp.sum(l_b[i], axis=0, keepdims=True)
            if grp >= 2:
                accs[(grp - 2) * g + i] = acc_c[i]
    return ls, accs


def _block_geometry(qi, tq, tk):
    nkb = ((qi + 1) * tq + tk - 1) // tk
    kl = lax.broadcasted_iota(I32, (tk, tq), 0)
    ql = lax.broadcasted_iota(I32, (tk, tq), 1)

    def koff(kb):
        return kb * tk - qi * tq + jnp.zeros((1, tq), I32)

    return nkb, kl, ql, koff


def _prompt_a_kernel(qt_ref, k_ref, vt_ref, qit_ref, wt_ref, ki_ref, o_ref,
                     sc_scr, s_scr, p_scr, ot_scr, *, tq, tk, topk):
    qi = pl.program_id(1)
    nkb, kl, ql, koff = _block_geometry(qi, tq, tk)
    last = nkb - 1
    causal_last = (kl + koff(last)) <= ql
    wt = wt_ref[0]
    k = float(topk)

    def idx_body(kb, c):
        ki = ki_ref[0, pl.ds(pl.multiple_of(kb * tk, tk), tk), :]
        acc = jnp.zeros((tk, tq), F32)
        for h in range(H_I):
            r = _dot(ki, qit_ref[0, h * D_I:(h + 1) * D_I, :])
            acc = acc + jnp.maximum(r, 0.0) * wt[h:h + 1, :]
        sc_scr[kb] = acc
        return c

    lax.fori_loop(0, nkb, idx_body, 0)
    sc_scr[last] = jnp.where(causal_last, sc_scr[last], NEG)

    def count_ge(v):
        def body(kb, cnt):
            t = sc_scr[kb]
            for i in range(tk // PART_ROWS):
                cnt = jnp.where(t[i * PART_ROWS:(i + 1) * PART_ROWS] >= v, cnt + 1.0, cnt)
            return cnt
        part = lax.fori_loop(0, nkb, body, jnp.zeros((PART_ROWS, tq), F32))
        return jnp.sum(part, axis=0, keepdims=True)

    def bit_body(i, prefix):
        bit = lax.shift_left(jnp.int32(1), jnp.int32(31) - i)
        cand = prefix | bit
        return jnp.where(count_ge(_float_of_key(cand)) >= k, cand, prefix)

    prefix = lax.fori_loop(0, 32, bit_body, jnp.zeros((1, tq), I32))
    thr = _float_of_key(prefix)
    tie = jnp.max(count_ge(thr)) > k

    @pl.when(jnp.logical_not(tie))
    def _():
        def sel_body(kb, c):
            sc_scr[kb] = jnp.where(sc_scr[kb] >= thr, 0.0, NEG)
            return c
        lax.fori_loop(0, nkb, sel_body, 0)

    @pl.when(tie)
    def _():
        def refine(_, c):
            lo, hi = c
            mid = lo + (hi - lo) * 0.5
            ok = count_ge(mid) >= k
            return jnp.where(ok, mid, lo), jnp.where(ok, hi, mid)

        lo, hi = lax.fori_loop(0, TIE_REFINE_STEPS, refine, (thr, _float_of_key(prefix + 1)))
        need = k - count_ge(hi)
        below = (lax.broadcasted_iota(I32, (tk, tk), 1)
                 < lax.broadcasted_iota(I32, (tk, tk), 0)).astype(BF16)

        def sel_body(kb, seen):
            t = sc_scr[kb]
            eq = (t >= lo) & (t < hi)
            eqf = jnp.where(eq, 1.0, 0.0)
            rank = _dot(below, eqf.astype(BF16)) + seen
            sc_scr[kb] = jnp.where((t >= hi) | (eq & (rank < need)), 0.0, NEG)
            return seen + jnp.sum(eqf, axis=0, keepdims=True)

        lax.fori_loop(0, nkb, sel_body, jnp.zeros((1, tq), F32))

    sc_scr[last] = jnp.where(causal_last, sc_scr[last], NEG)

    slopes = _slopes2(H_A)
    row = lax.broadcasted_iota(I32, (LANES, tq), 0)
    pos = _alibi_pos(tk)
    qts = []
    for h in range(H_A):
        pair, odd = h // 2, h % 2
        qp = qt_ref[0, pair * LANES:(pair + 1) * LANES, :]
        qts.append(jnp.concatenate(
            [jnp.where((row >= HD_A) == bool(odd), qp, jnp.zeros_like(qp)),
             _alibi_coef(slopes[h], tq)], axis=0))

    def scores(h, kb, is_last):
        cols = slice((h // 2) * LANES, (h // 2 + 1) * LANES)
        kblk = k_ref[0, pl.ds(pl.multiple_of(kb * tk, tk), tk), cols]
        return _dot(jnp.concatenate([kblk, pos], axis=1), qts[h]) + sc_scr[kb]

    def offset(h, kb):
        return slopes[h] * koff(kb).astype(F32)

    def values(h, kb):
        return vt_ref[0, kb, h * HD_A:(h + 1) * HD_A, :]

    ls, accs = _chain_pipeline(H_A, CHAINS_PER_STAGE, nkb, tq, HD_A, scores, offset, values,
                               s_scr, p_scr)
    for h in range(H_A):
        ot_scr[h * HD_A:(h + 1) * HD_A, :] = accs[h] / ls[h]
    o_ref[0] = ot_scr[...].T.astype(o_ref.dtype)


def _prompt_a(qat, ka, vta, qit, wt, ki, tq, tk, topk):
    b, t, _ = ka.shape
    nq, nk = t // tq, t // tk
    kern = functools.partial(_prompt_a_kernel, tq=tq, tk=tk, topk=topk)
    return pl.pallas_call(
        kern,
        grid=(b, nq),
        in_specs=[
            pl.BlockSpec((1, W_A, tq), lambda i, j: (i, 0, j)),
            pl.BlockSpec((1, t, W_A), lambda i, j: (i, 0, 0)),
            pl.BlockSpec((1, nk, W_A, tk), lambda i, j: (i, 0, 0, 0)),
            pl.BlockSpec((1, H_I * D_I, tq), lambda i, j: (i, 0, j)),
            pl.BlockSpec((1, H_I, tq), lambda i, j: (i, 0, j)),
            pl.BlockSpec((1, t, D_I), lambda i, j: (i, 0, 0)),
        ],
        out_specs=pl.BlockSpec((1, tq, W_A), lambda i, j: (i, j, 0)),
        out_shape=jax.ShapeDtypeStruct((b, t, W_A), F32),
        scratch_shapes=[
            pltpu.VMEM((nk, tk, tq), F32),
            pltpu.VMEM((CHAINS_PER_STAGE, nk, tk, tq), F32),
            pltpu.VMEM((CHAINS_PER_STAGE, nk, tk, tq), BF16),
            pltpu.VMEM((W_A, tq), F32),
        ],
        compiler_params=_cparams(("parallel", "arbitrary")),
        name="prompt_a",
    )(qat, ka, vta, qit, wt, ki)


def _lambda_of(lq1_ref, lk1_ref, lq2_ref, lk2_ref, lambda_init):
    s1 = jnp.sum(lq1_ref[...] * lk1_ref[...], axis=1, keepdims=True)
    s2 = jnp.sum(lq2_ref[...] * lk2_ref[...], axis=1, keepdims=True)
    return jnp.exp(s1) - jnp.exp(s2) + lambda_init


def _prompt_b_kernel(qt_ref, k_ref, vt_ref, lq1_ref, lk1_ref, lq2_ref, lk2_ref, g_ref, o_ref,
                     s_scr, p_scr, ot_scr, *, tq, tk, lambda_init):
    qi = pl.program_id(1)
    nkb, kl, ql, koff = _block_geometry(qi, tq, tk)
    causal_last = (kl + koff(nkb - 1)) <= ql
    lam = _lambda_of(lq1_ref, lk1_ref, lq2_ref, lk2_ref, lambda_init)
    slopes = _slopes2(H_B)
    row = lax.broadcasted_iota(I32, (LANES, tq), 0)
    pos = _alibi_pos(tk)
    qts = []
    for h in range(H_B):
        qp = qt_ref[0, h * LANES:(h + 1) * LANES, :]
        coef = _alibi_coef(slopes[h], tq)
        qts.append(jnp.concatenate([jnp.where(row < HD_B, qp, jnp.zeros_like(qp)), coef], axis=0))
        qts.append(jnp.concatenate([jnp.where(row >= HD_B, qp, jnp.zeros_like(qp)), coef], axis=0))

    def scores(c, kb, is_last):
        h = c // 2
        kblk = k_ref[0, pl.ds(pl.multiple_of(kb * tk, tk), tk), h * LANES:(h + 1) * LANES]
        s = _dot(jnp.concatenate([kblk, pos], axis=1), qts[c])
        return jnp.where(causal_last, s, NEG) if is_last else s

    def offset(c, kb):
        return slopes[c // 2] * koff(kb).astype(F32)

    def values(c, kb):
        h = c // 2
        return vt_ref[0, kb, h * LANES:(h + 1) * LANES, :]

    ls, accs = _chain_pipeline(2 * H_B, CHAINS_PER_STAGE, nkb, tq, LANES, scores, offset, values,
                               s_scr, p_scr, peel_last=True)
    g = g_ref[...]
    for h in range(H_B):
        o = accs[2 * h] / ls[2 * h] - lam * (accs[2 * h + 1] / ls[2 * h + 1])
        ms = jnp.mean(o * o, axis=0, keepdims=True)
        o = o * lax.rsqrt(ms + RMS_EPS) * g
        ot_scr[h * LANES:(h + 1) * LANES, :] = o * (1.0 - lambda_init)
    o_ref[0] = ot_scr[...].T.astype(o_ref.dtype)


def _prompt_b(qbt, kb, vtb, lq1, lk1, lq2, lk2, g_col, tq, tk, lambda_init):
    b, t, _ = kb.shape
    nq, nk = t // tq, t // tk
    kern = functools.partial(_prompt_b_kernel, tq=tq, tk=tk, lambda_init=lambda_init)
    small = lambda a: pl.BlockSpec(a.shape, lambda i, j: (0, 0))
    return pl.pallas_call(
        kern,
        grid=(b, nq),
        in_specs=[
            pl.BlockSpec((1, W_B, tq), lambda i, j: (i, 0, j)),
            pl.BlockSpec((1, t, W_B), lambda i, j: (i, 0, 0)),
            pl.BlockSpec((1, nk, W_B, tk), lambda i, j: (i, 0, 0, 0)),
            small(lq1), small(lk1), small(lq2), small(lk2), small(g_col),
        ],
        out_specs=pl.BlockSpec((1, tq, W_B), lambda i, j: (i, j, 0)),
        out_shape=jax.ShapeDtypeStruct((b, t, W_B), F32),
        scratch_shapes=[
            pltpu.VMEM((CHAINS_PER_STAGE, nk, tk, tq), F32),
            pltpu.VMEM((CHAINS_PER_STAGE, nk, tk, tq), BF16),
            pltpu.VMEM((W_B, tq), F32),
        ],
        compiler_params=_cparams(("parallel", "arbitrary")),
        name="prompt_b",
    )(qbt, kb, vtb, lq1, lk1, lq2, lk2, g_col)


def _sigmoid(x):
    return 1.0 / (1.0 + jnp.exp(-x))


def _merge_kernel(x_ref, oa_ref, ob_ref, wz_ref, bz_ref, wpa_ref, wpb_ref, wout_ref,
                  lng_ref, lnb_ref, y_ref, *, alpha):
    x = x_ref[...]
    xh = x.astype(BF16)

    def proj(c0, n):
        return _dot_nt(xh, wz_ref[c0:c0 + n, :]) + bz_ref[:, c0:c0 + n]

    d = x.shape[1]
    za = proj(0, W_A)
    ua = (oa_ref[...] * (za * _sigmoid(za))).astype(BF16)
    ya = _dot(ua, wpa_ref[...])
    zb = proj(W_A, W_B)
    ub = (ob_ref[...] * (zb * _sigmoid(zb))).astype(BF16)
    yb = _dot(ub, wpb_ref[...])
    m = _sigmoid(proj(W_A + W_B, d)) * ya + _sigmoid(proj(W_A + W_B + d, d)) * yb
    r = alpha * x + _dot(m.astype(BF16), wout_ref[...])
    mu = jnp.mean(r, axis=1, keepdims=True)
    rc = r - mu
    var = jnp.mean(rc * rc, axis=1, keepdims=True)
    y_ref[...] = rc * lax.rsqrt(var + LN_EPS) * lng_ref[...] + lnb_ref[...]


def _merge(x2d, oa, ob, wz, bz, wpa, wpb, wout, lng, lnb, tm, alpha):
    n, d = x2d.shape
    row = lambda w: pl.BlockSpec((tm, w), lambda i: (i, 0))
    full = lambda a: pl.BlockSpec(a.shape, lambda i: (0, 0))
    return pl.pallas_call(
        functools.partial(_merge_kernel, alpha=alpha),
        grid=(n // tm,),
        in_specs=[row(d), row(W_A), row(W_B), full(wz), full(bz), full(wpa), full(wpb),
                  full(wout), full(lng), full(lnb)],
        out_specs=row(d),
        out_shape=jax.ShapeDtypeStruct((n, d), F32),
        compiler_params=_cparams(("parallel",)),
        name="merge",
    )(x2d, oa, ob, wz, bz, wpa, wpb, wout, lng, lnb)


def _page_specs(n, shape, first_page):
    def spec(j):
        return pl.BlockSpec((1,) + shape, lambda i, s, pt, j=j: (pt[i, first_page(s) + j], 0, 0))
    return [spec(j) for j in range(n)]


def _sample_idx_kernel(pt_ref, qi_ref, wcol_ref, kin_ref, *rest, n_pages, pps, n_tok):
    kip_refs, sc_ref = rest[:pps], rest[pps]
    step = pl.program_id(1)
    n_steps = n_pages // pps
    lane = lax.broadcasted_iota(I32, (SUBLANES, PAGE), 1)
    trow = lax.broadcasted_iota(I32, (SUBLANES, PAGE), 0)

    def scores(kpages_t):
        r = _dot(qi_ref[0], kpages_t.astype(BF16))
        r = jnp.maximum(r, 0.0) * wcol_ref[0]
        per_tok = [jnp.sum(r[t * H_I:(t + 1) * H_I, :], axis=0, keepdims=True)
                   for t in range(n_tok)]
        pad = [jnp.zeros_like(per_tok[0])] * (SUBLANES - n_tok)
        return jnp.concatenate(per_tok + pad, axis=0)

    s_all = scores(jnp.concatenate([r[0] for r in kip_refs], axis=1))
    for j in range(pps):
        sc_ref[0, step * pps + j] = s_all[:, j * PAGE:(j + 1) * PAGE]

    @pl.when(step == n_steps - 1)
    def _():
        s = jnp.where(lane <= trow, scores(kin_ref[0]), NEG)
        sc_ref[0, n_pages] = jnp.where(lane < n_tok, s, PAD_SCORE)


def _sample_select_kernel(sc_ref, bias_ref, *, n_tok, topk):
    nb, n_tiles = sc_ref.shape[0], sc_ref.shape[1]
    rows = nb * SUBLANES
    lane = lax.broadcasted_iota(I32, (rows, PAGE), 1)
    trow = lax.broadcasted_iota(I32, (rows, PAGE), 0) % SUBLANES
    k = float(topk)
    tile = lambda j: sc_ref[:, j].reshape(rows, PAGE)

    def count_ge(v):
        cnt = jnp.zeros((rows, PAGE), F32)
        for j in range(n_tiles):
            cnt = jnp.where(tile(j) >= v, cnt + 1.0, cnt)
        return jnp.sum(cnt, axis=1, keepdims=True)

    lo, hi = _kth_largest_cell(count_ge, (rows, 1), k)
    need = k - count_ge(hi)
    before = (lax.broadcasted_iota(I32, (PAGE, PAGE), 0)
              < lax.broadcasted_iota(I32, (PAGE, PAGE), 1)).astype(BF16)
    seen = jnp.zeros((rows, 1), F32)
    for j in range(n_tiles):
        t = tile(j)
        eq = (t >= lo) & (t < hi)
        eqf = jnp.where(eq, 1.0, 0.0)
        rank = _dot(eqf.astype(BF16), before) + seen
        sel = (t >= hi) | (eq & (rank < need))
        if j == n_tiles - 1:
            sel = sel & (lane <= trow) & (lane < n_tok)
        bias_ref[:, j] = jnp.where(sel, 0.0, NEG).reshape(nb, SUBLANES, PAGE)
        seen = seen + jnp.sum(eqf, axis=1, keepdims=True)


def _sample_select(scores, n_tok, topk):
    return pl.pallas_call(
        functools.partial(_sample_select_kernel, n_tok=n_tok, topk=topk),
        out_shape=jax.ShapeDtypeStruct(scores.shape, F32),
        compiler_params=pltpu.CompilerParams(vmem_limit_bytes=VMEM_LIMIT),
        name="sample_select",
    )(scores)


def _sample_idx(page_table, qi_rows, wcol, cache_ki_t, ki_new_t, n_tok):
    b, n_pages = page_table.shape
    rows = n_tok * H_I
    pps = min(IDX_PAGES_PER_STEP, n_pages)
    kern = functools.partial(_sample_idx_kernel, n_pages=n_pages, pps=pps, n_tok=n_tok)
    grid_spec = pltpu.PrefetchScalarGridSpec(
        num_scalar_prefetch=1,
        grid=(b, n_pages // pps),
        in_specs=[
            pl.BlockSpec((1, rows, D_I), lambda i, s, pt: (i, 0, 0)),
            pl.BlockSpec((1, rows, 1), lambda i, s, pt: (i, 0, 0)),
            pl.BlockSpec((1, D_I, PAGE), lambda i, s, pt: (i, 0, 0)),
        ] + _page_specs(pps, (D_I, PAGE), lambda s: s * pps),
        out_specs=pl.BlockSpec((1, n_pages + 1, SUBLANES, PAGE), lambda i, s, pt: (i, 0, 0, 0)),
    )
    return pl.pallas_call(
        kern,
        grid_spec=grid_spec,
        out_shape=jax.ShapeDtypeStruct((b, n_pages + 1, SUBLANES, PAGE), F32),
        compiler_params=_cparams(("parallel", "arbitrary")),
        name="sample_idx",
    )(page_table, qi_rows, wcol, ki_new_t, *([cache_ki_t] * pps))


def _softmax_update(s, m_scr, l_scr):
    m = m_scr[...]
    m_new = jnp.maximum(m, jnp.max(s, axis=1, keepdims=True))
    p = jnp.exp2(s - m_new)
    alpha = jnp.exp2(m - m_new)
    l_scr[...] = alpha * l_scr[...] + jnp.sum(p, axis=1, keepdims=True)
    m_scr[...] = m_new
    return p, alpha


def _decode_a_parts(step, wq_ref, slope_ref, kn_ref, vn_ref, bias_ref, biasn_ref, kp_refs, vp_refs,
                    o_ref, acc_scr, m_scr, l_scr, *, pps, n_tok, past_len):
    rows = n_tok * H_A

    def init():
        acc_scr[...] = jnp.zeros_like(acc_scr)
        m_scr[...] = jnp.full_like(m_scr, NEG)
        l_scr[...] = jnp.zeros_like(l_scr)

    def update(k_ts, v_ts, first_pos, b8s, new_keys):
        n = len(k_ts)
        lane = lax.broadcasted_iota(I32, (rows, n * PAGE), 1)
        trow = lax.broadcasted_iota(I32, (rows, n * PAGE), 0) // H_A
        s = jnp.concatenate([_dot(wq_ref[0], k_t.astype(BF16)) for k_t in k_ts], axis=1)
        s = s - slope_ref[...] * (past_len + trow - (first_pos + lane)).astype(F32)
        s = s + jnp.concatenate(
            [jnp.broadcast_to(jnp.concatenate([b8[t:t + 1, :] for b8 in b8s], axis=1),
                              (H_A, n * PAGE)) for t in range(n_tok)], axis=0)
        if new_keys:
            s = jnp.where((lane <= trow) & (lane < n_tok), s, NEG)
        p, alpha = _softmax_update(s, m_scr, l_scr)
        p = p.astype(BF16)
        pv = _dot_nt(p[:, :PAGE], v_ts[0].astype(BF16))
        for j in range(1, n):
            pv = pv + _dot_nt(p[:, j * PAGE:(j + 1) * PAGE], v_ts[j].astype(BF16))
        acc_scr[...] = acc_scr[...] * alpha + pv

    def main():
        update([r[0] for r in kp_refs], [r[0] for r in vp_refs], step * (pps * PAGE),
               [bias_ref[0, j] for j in range(pps)], False)

    def final():
        update([kn_ref[0]], [vn_ref[0]], past_len, [biasn_ref[0, 0]], True)
        o = acc_scr[...] / l_scr[...]
        own = (lax.broadcasted_iota(I32, (rows, W_A), 1) // HD_A
               == lax.broadcasted_iota(I32, (rows, W_A), 0) % H_A)
        o = jnp.where(own, o, 0.0)
        o_ref[0] = jnp.concatenate(
            [jnp.sum(o[t * H_A:(t + 1) * H_A, :], axis=0, keepdims=True) for t in range(n_tok)],
            axis=0)

    return init, main, final


def _decode_b_parts(step, wq_ref, kn_ref, vn_ref, lq1_ref, lk1_ref, lq2_ref, lk2_ref, g_ref,
                    kp_refs, vp_refs, o_ref, acc_scr, m_scr, l_scr, *, pps, n_tok, past_len,
                    lambda_init):
    hrows = 2 * n_tok
    rows = H_B * hrows
    slope_col = jnp.concatenate([jnp.full((hrows, 1), s, F32) for s in _slopes2(H_B)], axis=0)

    def init():
        acc_scr[...] = jnp.zeros_like(acc_scr)
        m_scr[...] = jnp.full_like(m_scr, NEG)
        l_scr[...] = jnp.zeros_like(l_scr)

    def update(k_hs, v_hs, first_pos, new_keys):
        n = len(k_hs[0])
        lane = lax.broadcasted_iota(I32, (rows, n * PAGE), 1)
        trow = lax.broadcasted_iota(I32, (rows, n * PAGE), 0) % n_tok
        s = jnp.concatenate(
            [jnp.concatenate([_dot_nt(wq_ref[0, h], k.astype(BF16)) for k in k_hs[h]], axis=1)
             for h in range(H_B)], axis=0)
        s = s - slope_col * (past_len + trow - (first_pos + lane)).astype(F32)
        if new_keys:
            s = jnp.where((lane <= trow) & (lane < n_tok), s, NEG)
        p, alpha = _softmax_update(s, m_scr, l_scr)
        p = p.astype(BF16)
        pvs = []
        for h in range(H_B):
            ph = p[h * hrows:(h + 1) * hrows, :]
            pv = _dot(ph[:, :PAGE], v_hs[h][0].astype(BF16))
            for j in range(1, n):
                pv = pv + _dot(ph[:, j * PAGE:(j + 1) * PAGE], v_hs[h][j].astype(BF16))
            pvs.append(pv)
        acc_scr[...] = acc_scr[...] * alpha + jnp.concatenate(pvs, axis=0)

    head_rows = lambda refs, h: [r[0, pl.ds(h, PAGE, stride=H_B), :] for r in refs]

    def main():
        update([head_rows(kp_refs, h) for h in range(H_B)],
               [head_rows(vp_refs, h) for h in range(H_B)], step * (pps * PAGE), False)

    def final():
        update([[kn_ref[0, h]] for h in range(H_B)], [[vn_ref[0, h]] for h in range(H_B)],
               past_len, True)
        lam = _lambda_of(lq1_ref, lk1_ref, lq2_ref, lk2_ref, lambda_init)
        o_all = acc_scr[...] / l_scr[...]
        for h in range(H_B):
            o = (o_all[h * hrows:h * hrows + n_tok, :]
                 - lam * o_all[h * hrows + n_tok:(h + 1) * hrows, :])
            ms = jnp.mean(o * o, axis=1, keepdims=True)
            o = o * lax.rsqrt(ms + RMS_EPS) * g_ref[...] * (1.0 - lambda_init)
            o_ref[0, :, h * LANES:(h + 1) * LANES] = o

    return init, main, final


N_A_IN, N_B_IN = 6, 8


def _decode_kernel(pt_ref, *refs, n_pages, pps, n_tok, past_len, lambda_init):
    a_in, b_in = refs[:N_A_IN], refs[N_A_IN:N_A_IN + N_B_IN]
    pages = refs[N_A_IN + N_B_IN:N_A_IN + N_B_IN + 4 * pps]
    o_a, o_b, acc_a, m_a, l_a, acc_b, m_b, l_b = refs[N_A_IN + N_B_IN + 4 * pps:]
    ka, va, kb, vb = (pages[i * pps:(i + 1) * pps] for i in range(4))
    step = pl.program_id(1)
    init_a, main_a, final_a = _decode_a_parts(step, *a_in, ka, va, o_a, acc_a, m_a, l_a,
                                              pps=pps, n_tok=n_tok, past_len=past_len)
    init_b, main_b, final_b = _decode_b_parts(step, *b_in, kb, vb, o_b, acc_b, m_b, l_b,
                                              pps=pps, n_tok=n_tok, past_len=past_len,
                                              lambda_init=lambda_init)

    @pl.when(step == 0)
    def _():
        init_a()
        init_b()

    main_a()
    main_b()

    @pl.when(step == n_pages // pps - 1)
    def _():
        final_a()
        final_b()


def _decode(page_table, wq_a, slope_col, k_a_t, v_a_t, k_new_a, v_new_a, bias,
            wq_b, k_b_r, v_b_r, k_new_b, v_new_b, lam_args, g_row, n_tok, past_len, lambda_init):
    b, n_pages = page_table.shape
    rows_a, rows_b = n_tok * H_A, H_B * 2 * n_tok
    pps = min(PAGES_PER_STEP, n_pages)
    kern = functools.partial(_decode_kernel, n_pages=n_pages, pps=pps, n_tok=n_tok,
                             past_len=past_len, lambda_init=lambda_init)
    pages_a = _page_specs(pps, (W_A, PAGE), lambda s: s * pps)
    pages_b = _page_specs(pps, (PAGE * H_B, 2 * HD_B), lambda s: s * pps)
    per_batch = lambda shape: pl.BlockSpec((1,) + shape, lambda i, s, pt: (i,) + (0,) * len(shape))
    small = lambda a: pl.BlockSpec(a.shape, lambda i, s, pt: (0, 0))
    grid_spec = pltpu.PrefetchScalarGridSpec(
        num_scalar_prefetch=1,
        grid=(b, n_pages // pps),
        in_specs=[
            per_batch((rows_a, W_A)), small(slope_col),
            per_batch((W_A, PAGE)), per_batch((W_A, PAGE)),
            pl.BlockSpec((1, pps, SUBLANES, PAGE), lambda i, s, pt: (i, s, 0, 0)),
            pl.BlockSpec((1, 1, SUBLANES, PAGE), lambda i, s, pt: (i, n_pages, 0, 0)),
            per_batch((H_B, 2 * n_tok, 2 * HD_B)),
            per_batch((H_B, PAGE, 2 * HD_B)), per_batch((H_B, PAGE, 2 * HD_B)),
        ] + [small(a) for a in lam_args] + [small(g_row)] + pages_a + pages_a + pages_b + pages_b,
        out_specs=[per_batch((n_tok, W_A)), per_batch((n_tok, W_B))],
        scratch_shapes=[pltpu.VMEM((rows_a, W_A), F32), pltpu.VMEM((rows_a, 1), F32),
                        pltpu.VMEM((rows_a, 1), F32), pltpu.VMEM((rows_b, 2 * HD_B), F32),
                        pltpu.VMEM((rows_b, 1), F32), pltpu.VMEM((rows_b, 1), F32)],
    )
    return pl.pallas_call(
        kern,
        grid_spec=grid_spec,
        out_shape=[jax.ShapeDtypeStruct((b, n_tok, W_A), F32),
                   jax.ShapeDtypeStruct((b, n_tok, W_B), F32)],
        compiler_params=_cparams(("parallel", "arbitrary")),
        name="decode",
    )(page_table, wq_a, slope_col, k_new_a, v_new_a, bias, bias, wq_b, k_new_b, v_new_b,
      *lam_args, g_row, *([k_a_t] * pps), *([v_a_t] * pps), *([k_b_r] * pps), *([v_b_r] * pps))


def _split_rows(wt, d_model):
    out, off = {}, 0
    for name, n in SPLITS:
        n = d_model if n is None else n
        out[name] = wt[off:off + n]
        off += n
    return out


def _pad_rows(a, rows):
    return jnp.pad(a, [(0, rows - a.shape[0])] + [(0, 0)] * (a.ndim - 1))


def _proj_weights(w_in, b_in, d_model):
    w = _split_rows(w_in.T, d_model)
    b = _split_rows(b_in, d_model)
    order1 = ['q_a', 'k_a', 'v_a', 'q_b', 'k_b', 'v_b', 'q_idx']
    group1 = lambda p: jnp.concatenate(
        [p[n] for n in order1] + [_pad_rows(p['k_idx'], LANES), _pad_rows(p['w_idx'], LANES)])
    order2 = ['z_a', 'z_b', 'g_a', 'g_b']
    group2 = lambda p: jnp.concatenate([p[n] for n in order2])
    return (group1(w).astype(BF16), group1(b)[None, :], group2(w).astype(BF16),
            group2(b)[None, :])


def _prompt_branches(xp, bp, t, w1t, b1, lam_args, g, tq, tk, topk, lambda_init):
    (qat, kat32, ka16, vat32, vat, qbt, kb32, kb16, vb32, vbt, qit, kit32, ki16, wit) = _project(
        xp, w1t, b1, tk, batch=bp)
    r3 = lambda a: a.reshape(bp, t, a.shape[-1])
    o_a = _prompt_a(qat, r3(ka16), vat, qit, wit, r3(ki16), tq, tk, topk)
    o_b = _prompt_b(qbt, r3(kb16), vbt, *lam_args, g[:, None], tq, tk, lambda_init)
    kv = (kat32.reshape(bp, H_A, HD_A, t).transpose(0, 3, 1, 2),
          vat32.reshape(bp, H_A, HD_A, t).transpose(0, 3, 1, 2),
          kit32.transpose(0, 2, 1),
          kb32.reshape(bp, t, H_B, 2 * HD_B), vb32.reshape(bp, t, H_B, 2 * HD_B))
    return o_a.reshape(bp * t, W_A), o_b.reshape(bp * t, W_B), kv


def _sample_branches(xs, bs, ts, page_table, c_k_a, c_v_a, c_k_idx, c_k_b, c_v_b, w1t, b1,
                     lam_args, g, topk, lambda_init):
    n_pool = c_k_a.shape[0]
    past_len = page_table.shape[1] * PAGE
    qa, ka32, va32, qb, kb32, vb32, qidx, ki32, wi = _project(xs, w1t, b1, bs * ts)
    s3 = lambda a: a.reshape(bs, ts, a.shape[-1])
    new_t = lambda a: jnp.pad(s3(a).transpose(0, 2, 1), ((0, 0), (0, 0), (0, PAGE - ts)))
    qi_rows = s3(qidx).reshape(bs, ts * H_I, D_I)
    wcol = s3(wi).reshape(bs, ts * H_I, 1)
    scores = _sample_idx(page_table, qi_rows, wcol, c_k_idx.transpose(0, 2, 1), new_t(ki32), ts)
    bias = _sample_select(scores, ts, topk)
    eye_a = jnp.eye(H_A, dtype=BF16)
    wq_a = (s3(qa).reshape(bs, ts, H_A, 1, HD_A) * eye_a[None, None, :, :, None]
            ).reshape(bs, ts * H_A, W_A)
    slope_a = jnp.tile(jnp.asarray(_slopes2(H_A), F32), ts)[:, None]
    page_t = lambda c: c.transpose(0, 2, 3, 1).reshape(n_pool, W_A, PAGE)
    qb4 = s3(qb).reshape(bs, ts, H_B, 2 * HD_B).transpose(0, 2, 1, 3)
    first = (jnp.arange(2 * HD_B) < HD_B)
    wq_b = jnp.concatenate([jnp.where(first, qb4, 0), jnp.where(first, 0, qb4)], axis=2)
    new_h = lambda a: jnp.pad(a.reshape(bs, ts, H_B, 2 * HD_B).transpose(0, 2, 1, 3),
                              ((0, 0), (0, 0), (0, PAGE - ts), (0, 0)))
    page_r = lambda c: c.reshape(n_pool, PAGE * H_B, 2 * HD_B)
    o_a, o_b = _decode(page_table, wq_a, slope_a, page_t(c_k_a), page_t(c_v_a), new_t(ka32),
                       new_t(va32), bias, wq_b, page_r(c_k_b), page_r(c_v_b), new_h(kb32),
                       new_h(vb32), lam_args, g[None, :], ts, past_len, lambda_init)
    return (o_a.reshape(bs * ts, W_A), o_b.reshape(bs * ts, W_B),
            (ka32, va32, ki32, kb32, vb32))


def kernel(x_prompt, x_sample, cache_k_a, cache_v_a, cache_k_idx, cache_k_b, cache_v_b, page_table,
           w_in, b_in, lambda_q1, lambda_k1, lambda_q2, lambda_k2, subln_g, w_pa, w_pb, w_out,
           ln_g, ln_b):
    depth = w_in.shape[0]
    bp, t, d = x_prompt.shape
    bs, ts, _ = x_sample.shape
    past_len = page_table.shape[1] * PAGE
    alpha = (2.0 * depth) ** 0.25
    tq = min(256, t)
    tk = min(512, t)
    tm = min(512, bp * t)
    topk_p = min(TOPK_MAX, t // 4)
    topk_s = min(TOPK_MAX, (past_len + ts) // 4)

    xp = x_prompt.reshape(bp * t, d)
    xs = x_sample.reshape(bs * ts, d)
    pk, sk = [], []
    for l in range(depth):
        lambda_init = 0.8 - 0.6 * math.exp(-0.3 * l)
        w1t, b1, wzt, bz = _proj_weights(w_in[l], b_in[l], d)
        lam_args = (lambda_q1[l][None, :], lambda_k1[l][None, :],
                    lambda_q2[l][None, :], lambda_k2[l][None, :])
        merge_w = (wzt, bz, w_pa[l].astype(BF16), w_pb[l].astype(BF16), w_out[l].astype(BF16),
                   ln_g[l][None, :], ln_b[l][None, :])
        o_a, o_b, kv = _prompt_branches(xp, bp, t, w1t, b1, lam_args, subln_g[l], tq, tk, topk_p,
                                        lambda_init)
        pk.append(kv)
        xp = _merge(xp, o_a, o_b, *merge_w, tm, alpha)
        o_a, o_b, kv = _sample_branches(xs, bs, ts, page_table, cache_k_a[l], cache_v_a[l],
                                        cache_k_idx[l], cache_k_b[l], cache_v_b[l], w1t, b1,
                                        lam_args, subln_g[l], topk_s, lambda_init)
        sk.append(kv)
        xs = _merge(xs, o_a, o_b, *merge_w, bs * ts, alpha)

    def stacked(layers, b, n):
        shapes = ((b, n, H_A, HD_A), (b, n, H_A, HD_A), (b, n, D_I),
                  (b, n, H_B, 2 * HD_B), (b, n, H_B, 2 * HD_B))
        return tuple(jnp.stack([kv[i].reshape(s) for kv in layers]) for i, s in enumerate(shapes))

    return (xp.reshape(bp, t, d), xs.reshape(bs, ts, d)) + stacked(pk, bp, t) + stacked(sk, bs, ts)
```
